```python
import math
import jax
import jax.numpy as jnp
from jax import lax
import numpy as np

D_MODEL = 2048
BATCH = 4
SEQ = 4096
DEPTH = 2

HEAD_DIM = 128
A_HEADS = 6
B_HEADS = 4
C_HEADS = 6
B_QK_DIM = 64
DILATED_CONFIGS = ((128, 1), (512, 4), (2048, 16))
Q_BLOCK = 128
MLSTM_CHUNK = 128
CONV_WIDTH = 3
MEM_LEN = 256
X_HEADS = 4
X_HEAD_DIM = D_MODEL // X_HEADS
D_FF = 4 * D_MODEL
FORGET_BIAS = 3.0
EPS = 1e-6
NEG = -1e30

A_W = A_HEADS * HEAD_DIM
B_QK_W = B_HEADS * 2 * B_QK_DIM
B_V_W = B_HEADS * HEAD_DIM
C_W = C_HEADS * HEAD_DIM
N_GATES = 4 * C_HEADS
N_MIX_HEADS = A_HEADS + B_HEADS + C_HEADS
MIX_W = N_MIX_HEADS * HEAD_DIM
IN_SPLITS = (A_W, A_W, A_W, B_QK_W, B_QK_W, B_V_W, 2 * C_W, C_W, C_W, N_GATES)
IN_W = 3 * A_W + 2 * B_QK_W + B_V_W + 4 * C_W + N_GATES
F32 = jnp.float32

kernel_name = 'hybrid_dilated_diff_mlstm_encoder'


def rmsnorm(x, g):
    xf = x.astype(F32)
    y = xf * lax.rsqrt(jnp.mean(xf * xf, axis=-1, keepdims=True) + EPS)
    return (y * g.astype(F32)).astype(x.dtype)


def alibi_slopes(n_heads):
    return jnp.asarray(2.0 ** (-8.0 * np.arange(1, n_heads + 1) / n_heads), dtype=F32)


def split_heads(u, n_heads):
    b, t, _ = u.shape
    return u.reshape(b, t, n_heads, -1).transpose(0, 2, 1, 3)


def centred_conv(u, w):
    k, t = w.shape[0], u.shape[1]
    left = k // 2
    up = jnp.pad(u, ((0, 0), (left, k - 1 - left), (0, 0)))
    out = up[:, 0:t] * w[0]
    for j in range(1, k):
        out = out + up[:, j:j + t] * w[j]
    return out


def dilated_window_attn(q, k, v, slopes, window, dilation):
    b, h, t, dh = q.shape
    r = dilation
    half = window // (2 * r)
    n_sub = t // r
    nb = -(-n_sub // half)
    lp = nb * half

    def to_sub(u):
        return u.reshape(b, h, n_sub, r, dh).transpose(0, 1, 3, 2, 4)

    qb = jnp.pad(to_sub(q), ((0, 0), (0, 0), (0, 0), (0, lp - n_sub), (0, 0))).reshape(b, h, r, nb, half, dh)

    def band(u):
        up = jnp.pad(to_sub(u), ((0, 0), (0, 0), (0, 0), (half, lp - n_sub + half), (0, 0)))
        up = up.reshape(b, h, r, nb + 2, half, dh)
        return jnp.concatenate([up[:, :, :, 0:nb], up[:, :, :, 1:nb + 1], up[:, :, :, 2:nb + 2]], axis=4)

    kb, vb = band(k), band(v)
    qi = jnp.arange(nb)[:, None] * half + jnp.arange(half)[None, :]
    ki = jnp.arange(nb)[:, None] * half + jnp.arange(3 * half)[None, :] - half
    delta = jnp.abs(qi[:, :, None] - ki[:, None, :])
    valid = (delta <= half) & (ki[:, None, :] >= 0) & (ki[:, None, :] < n_sub)
    dist = (delta * r).astype(F32)
    s = jnp.einsum('bhrnqd,bhrnkd->bhrnqk', qb, kb).astype(F32) * (dh ** -0.5)
    s = s - slopes[None, :, None, None, None, None] * dist
    s = jnp.where(valid, s, NEG)
    m = jnp.max(s, axis=-1, keepdims=True)
    p = jnp.exp(s - m)
    den = jnp.sum(p, axis=-1)
    o = jnp.einsum('bhrnqk,bhrnkd->bhrnqd', p.astype(v.dtype), vb).astype(F32) / den[..., None]
    lse = m[..., 0] + jnp.log(den)
    o = o.reshape(b, h, r, lp, dh)[:, :, :, :n_sub].transpose(0, 1, 3, 2, 4).reshape(b, h, t, dh)
    lse = lse.reshape(b, h, r, lp)[:, :, :, :n_sub].transpose(0, 1, 3, 2).reshape(b, h, t)
    return o, lse


def dilated_mixture(q, k, v, slopes):
    outs, lses = [], []
    for window, dilation in DILATED_CONFIGS:
        o, lse = dilated_window_attn(q, k, v, slopes, window, dilation)
        outs.append(o)
        lses.append(lse)
    wts = jax.nn.softmax(jnp.stack(lses, axis=0), axis=0)
    return jnp.einsum('gbht,gbhtd->bhtd', wts, jnp.stack(outs, axis=0))


def diff_attention(q1, q2, k1, k2, v, lam, slopes):
    b, h, t, d = q1.shape
    nq = t // Q_BLOCK
    scale = d ** -0.5
    kpos = jnp.arange(t)

    def blocks(u):
        return u.reshape(b, h, nq, Q_BLOCK, d).transpose(2, 0, 1, 3, 4)

    def one_block(args):
        qa, qc, start = args
        qpos = start + jnp.arange(Q_BLOCK)
        bias = -slopes[:, None, None] * jnp.abs(qpos[:, None] - kpos[None, :]).astype(F32)
        s1 = jnp.einsum('bhqd,bhkd->bhqk', qa, k1).astype(F32) * scale + bias
        s2 = jnp.einsum('bhqd,bhkd->bhqk', qc, k2).astype(F32) * scale + bias
        a = jax.nn.softmax(s1, axis=-1) - lam * jax.nn.softmax(s2, axis=-1)
        return jnp.einsum('bhqk,bhkd->bhqd', a.astype(v.dtype), v)

    o = lax.map(one_block, (blocks(q1), blocks(q2), jnp.arange(nq) * Q_BLOCK))
    return o.transpose(1, 2, 0, 3, 4).reshape(b, h, t, v.shape[-1])


def mlstm_chunkwise(q, k, v, log_i, log_f):
    b, nh, t, d = q.shape
    lc = MLSTM_CHUNK
    nc = t // lc

    def chunks(u):
        return jnp.moveaxis(u.reshape((b, nh, nc, lc) + u.shape[3:]), 2, 0)

    q = q.astype(F32) * (d ** -0.5)
    k = k.astype(F32)
    v = v.astype(F32)
    lower = jnp.tril(jnp.ones((lc, lc), dtype=bool))

    def step(carry, xs):
        c_state, n_state, m_state = carry
        qc, kc, vc, li, lf = xs
        bcum = jnp.cumsum(lf, axis=-1)
        dmat = jnp.where(lower, bcum[..., :, None] - bcum[..., None, :] + li[..., None, :], NEG)
        inter = bcum + m_state[..., None]
        m_t = jnp.maximum(inter, jnp.max(dmat, axis=-1))
        dw = jnp.exp(dmat - m_t[..., None])
        iw = jnp.exp(inter - m_t)
        sc = jnp.einsum('bhtd,bhsd->bhts', qc, kc) * dw
        num = iw[..., None] * jnp.einsum('bhtd,bhde->bhte', qc, c_state) + jnp.einsum('bhts,bhse->bhte', sc, vc)
        den = iw * jnp.einsum('bhtd,bhd->bht', qc, n_state) + jnp.sum(sc, axis=-1)
        h_out = num / jnp.maximum(jnp.abs(den), jnp.exp(-m_t))[..., None]
        b_last = bcum[..., -1]
        g = b_last[..., None] - bcum + li
        m_new = jnp.maximum(b_last + m_state, jnp.max(g, axis=-1))
        decay = jnp.exp(b_last + m_state - m_new)
        gw = jnp.exp(g - m_new[..., None])
        c_new = decay[..., None, None] * c_state + jnp.einsum('bhs,bhsd,bhse->bhde', gw, kc, vc)
        n_new = decay[..., None] * n_state + jnp.einsum('bhs,bhsd->bhd', gw, kc)
        return (c_new, n_new, m_new), h_out

    init = (jnp.zeros((b, nh, d, d), F32), jnp.zeros((b, nh, d), F32), jnp.zeros((b, nh), F32))
    _, hs = lax.scan(step, init, (chunks(q), chunks(k), chunks(v), chunks(log_i), chunks(log_f)))
    return jnp.moveaxis(hs, 0, 2).reshape(b, nh, t, d)


def bidirectional_mlstm(q, k, v, li_f, lf_f, li_b, lf_b):
    fwd = mlstm_chunkwise(q, k, v, li_f, lf_f)
    flip = lambda u: jnp.flip(u, axis=2)
    bwd = flip(mlstm_chunkwise(flip(q), flip(k), flip(v), flip(li_b), flip(lf_b)))
    return fwd + bwd


def hybrid_mixer(hn, layer, w_in, conv_w, gate_b, diff_lambda, head_norm_g, w_out):
    b, t, _ = hn.shape
    proj = hn @ w_in
    aq, ak, av, bq, bk, bv, cqk, cv, co, cg = jnp.split(proj, np.cumsum(IN_SPLITS)[:-1].tolist(), axis=-1)

    o_a = dilated_mixture(split_heads(aq, A_HEADS), split_heads(ak, A_HEADS), split_heads(av, A_HEADS),
                          alibi_slopes(A_HEADS))

    bq = bq.reshape(b, t, B_HEADS, 2, B_QK_DIM).transpose(3, 0, 2, 1, 4)
    bk = bk.reshape(b, t, B_HEADS, 2, B_QK_DIM).transpose(3, 0, 2, 1, 4)
    lam_init = 0.8 - 0.6 * math.exp(-0.3 * layer)
    lq1, lk1, lq2, lk2 = diff_lambda.astype(F32)
    lam = jnp.exp(jnp.sum(lq1 * lk1)) - jnp.exp(jnp.sum(lq2 * lk2)) + lam_init
    o_b = diff_attention(bq[0], bq[1], bk[0], bk[1], split_heads(bv, B_HEADS), lam, alibi_slopes(B_HEADS))

    cqk = jax.nn.silu(centred_conv(cqk, conv_w))
    cq, ck = jnp.split(cqk, 2, axis=-1)
    g = (cg.astype(F32) + gate_b.astype(F32)).reshape(b, t, 4, C_HEADS).transpose(2, 0, 3, 1)
    o_c = bidirectional_mlstm(split_heads(cq, C_HEADS), split_heads(ck, C_HEADS), split_heads(cv, C_HEADS),
                              g[0], jax.nn.log_sigmoid(g[1]), g[2], jax.nn.log_sigmoid(g[3]))
    o_c = o_c * jax.nn.sigmoid(split_heads(co, C_HEADS).astype(F32))

    heads = jnp.concatenate([o_a.astype(hn.dtype), o_b.astype(hn.dtype), o_c.astype(hn.dtype)], axis=1)
    heads = rmsnorm(heads, head_norm_g.reshape(N_MIX_HEADS, 1, HEAD_DIM))
    head_scale = np.ones(N_MIX_HEADS, np.float32)
    head_scale[A_HEADS:A_HEADS + B_HEADS] = 1.0 - lam_init
    heads = heads * jnp.asarray(head_scale, dtype=heads.dtype)[:, None, None]
    return heads.transpose(0, 2, 1, 3).reshape(b, t, MIX_W) @ w_out


def cross_attention(hn, mem_n, w_q, w_kv, w_o):
    b, t, _ = hn.shape
    m = mem_n.shape[1]
    q = (hn @ w_q).reshape(b, t, X_HEADS, X_HEAD_DIM)
    kv = (mem_n @ w_kv).reshape(b, m, 2, X_HEADS, X_HEAD_DIM)
    k, v = kv[:, :, 0], kv[:, :, 1]
    s = jnp.einsum('bthd,bmhd->bhtm', q, k).astype(F32) * (X_HEAD_DIM ** -0.5)
    a = jax.nn.softmax(s, axis=-1).astype(v.dtype)
    o = jnp.einsum('bhtm,bmhd->bthd', a, v).reshape(b, t, D_MODEL)
    return o @ w_o


def setup_inputs(seed: int = 0) -> dict:
    key = jax.random.key(seed)
    ks = jax.random.split(key, 18)

    def dense(k, shape, fan_in):
        return jax.random.normal(k, shape, F32) * (fan_in ** -0.5)

    def gain(k, shape):
        return 1.0 + 0.02 * jax.random.normal(k, shape, F32)

    gate_offset = jnp.asarray(np.array([0.0, FORGET_BIAS, 0.0, FORGET_BIAS], np.float32))[None, :, None]
    gate_b = (gate_offset + 0.1 * jax.random.normal(ks[5], (DEPTH, 4, C_HEADS), F32)).reshape(DEPTH, N_GATES)
    return {
        'x': jax.random.normal(ks[0], (BATCH, SEQ, D_MODEL), F32),
        'mem': jax.random.normal(ks[1], (BATCH, MEM_LEN, D_MODEL), F32),
        'norm_mix_g': gain(ks[2], (DEPTH, D_MODEL)),
        'w_in': dense(ks[3], (DEPTH, D_MODEL, IN_W), D_MODEL),
        'conv_w': dense(ks[4], (DEPTH, CONV_WIDTH, 2 * C_W), CONV_WIDTH),
        'gate_b': gate_b,
        'diff_lambda': 0.1 * jax.random.normal(ks[6], (DEPTH, 4, B_QK_DIM), F32),
        'head_norm_g': gain(ks[7], (DEPTH, MIX_W)),
        'w_out': dense(ks[8], (DEPTH, MIX_W, D_MODEL), MIX_W),
        'norm_x_g': gain(ks[9], (DEPTH, D_MODEL)),
        'norm_mem_g': gain(ks[10], (DEPTH, D_MODEL)),
        'w_xq': dense(ks[11], (DEPTH, D_MODEL, D_MODEL), D_MODEL),
        'w_xkv': dense(ks[12], (DEPTH, D_MODEL, 2 * D_MODEL), D_MODEL),
        'w_xo': dense(ks[13], (DEPTH, D_MODEL, D_MODEL), D_MODEL),
        'norm_mlp_g': gain(ks[14], (DEPTH, D_MODEL)),
        'w_up': dense(ks[15], (DEPTH, D_MODEL, D_FF), D_MODEL),
        'w_down': dense(ks[16], (DEPTH, D_FF, D_MODEL), D_FF),
        'final_norm_g': gain(ks[17], (D_MODEL,)),
    }


def reference(x, mem, norm_mix_g, w_in, conv_w, gate_b, diff_lambda, head_norm_g, w_out, norm_x_g, norm_mem_g,
              w_xq, w_xkv, w_xo, norm_mlp_g, w_up, w_down, final_norm_g):
    h = x
    for layer in range(DEPTH):
        h = h + hybrid_mixer(rmsnorm(h, norm_mix_g[layer]), layer, w_in[layer], conv_w[layer], gate_b[layer],
                             diff_lambda[layer], head_norm_g[layer], w_out[layer])
        h = h + cross_attention(rmsnorm(h, norm_x_g[layer]), rmsnorm(mem, norm_mem_g[layer]),
                                w_xq[layer], w_xkv[layer], w_xo[layer])
        u = rmsnorm(h, norm_mlp_g[layer]) @ w_up[layer]
        h = h + jnp.square(jax.nn.relu(u)) @ w_down[layer]
    return rmsnorm(h, final_norm_g)
```

```python
import functools
import math

import numpy as np
import jax
import jax.numpy as jnp
from jax import lax
from jax.experimental import pallas as pl
from jax.experimental.pallas import tpu as pltpu

F32 = jnp.float32
BF16 = jnp.bfloat16

D_MODEL = 2048
HEAD_DIM = 128
A_HEADS = 6
B_HEADS = 4
C_HEADS = 6
B_QK_DIM = 64
DILATED_CONFIGS = ((128, 1), (512, 4), (2048, 16))
HALF_WINDOW = 64
MLSTM_CHUNK = 128
X_HEADS = 4
X_HEAD_DIM = D_MODEL // X_HEADS
EPS = 1e-6
NEG = -1e30

A_W = A_HEADS * HEAD_DIM
B_QK_W = B_HEADS * 2 * B_QK_DIM
B_V_W = B_HEADS * HEAD_DIM
C_W = C_HEADS * HEAD_DIM
N_GATES = 4 * C_HEADS
N_MIX_HEADS = A_HEADS + B_HEADS + C_HEADS
MIX_W = N_MIX_HEADS * HEAD_DIM

P_W = 3 * A_W + 2 * B_QK_W + B_V_W + C_W
P_BQ = 3 * A_W
P_BK = P_BQ + B_QK_W
P_BV = P_BK + B_QK_W
P_CV = P_BV + B_V_W
PF_CO = 2 * C_W
PF_CG = PF_CO + C_W
PF_W = PF_CG + HEAD_DIM

LANE = 128
ROW_CHUNK = 16
VMEM_LIMIT = 56 * 1024 * 1024


def _params(n_axes, vmem=VMEM_LIMIT):
    return pltpu.CompilerParams(dimension_semantics=("arbitrary",) * n_axes, vmem_limit_bytes=vmem)


def _alibi_slopes(n_heads):
    return jnp.asarray(2.0 ** (-8.0 * np.arange(1, n_heads + 1) / n_heads), dtype=F32)


def _rmsnorm_rows_to(x_ref, g_ref, dst_ref):
    rows = x_ref.shape[0]

    def body(i, carry):
        r0 = pl.multiple_of(i * ROW_CHUNK, ROW_CHUNK)
        x = x_ref[pl.ds(r0, ROW_CHUNK), :]
        ms = jnp.mean(x * x, axis=-1, keepdims=True)
        dst_ref[pl.ds(r0, ROW_CHUNK), :] = (x * lax.rsqrt(ms + EPS) * g_ref[...]).astype(BF16)
        return carry

    lax.fori_loop(0, rows // ROW_CHUNK, body, 0)


def _norm_matmul_body(x_ref, g_ref, w_ref, o_ref, xn_ref):
    @pl.when(pl.program_id(1) == 0)
    def _():
        _rmsnorm_rows_to(x_ref, g_ref, xn_ref)

    o_ref[...] = jnp.dot(xn_ref[...], w_ref[...], preferred_element_type=F32).astype(o_ref.dtype)


def _norm_matmul(x, g, w, out_dtype, tm, tn):
    m, k = x.shape
    n = w.shape[1]
    return pl.pallas_call(
        _norm_matmul_body,
        grid=(m // tm, n // tn),
        in_specs=[
            pl.BlockSpec((tm, k), lambda i, j: (i, 0)),
            pl.BlockSpec((1, k), lambda i, j: (0, 0)),
            pl.BlockSpec((k, tn), lambda i, j: (0, j)),
        ],
        out_specs=pl.BlockSpec((tm, tn), lambda i, j: (i, j)),
        out_shape=jax.ShapeDtypeStruct((m, n), out_dtype),
        scratch_shapes=[pltpu.VMEM((tm, k), BF16)],
        compiler_params=_params(2),
        name="norm_matmul",
    )(x, g.reshape(1, k), w)


CONV_COLS = 256
CONV_PAD = 8


def _conv_silu_body(x_ref, w_ref, o_ref, pad_ref):
    t = x_ref.shape[0]
    zeros = jnp.zeros((CONV_PAD, CONV_COLS), F32)
    pad_ref[0:CONV_PAD, :] = zeros
    pad_ref[CONV_PAD + t:2 * CONV_PAD + t, :] = zeros
    scale = jnp.where(pl.program_id(1) < C_W // CONV_COLS, HEAD_DIM ** -0.5, 1.0).astype(F32)

    def copy(i, carry):
        r0 = pl.multiple_of(i * LANE, LANE)
        pad_ref[pl.ds(CONV_PAD + r0, LANE), :] = x_ref[pl.ds(r0, LANE), :]
        return carry

    lax.fori_loop(0, t // LANE, copy, 0)
    w = w_ref[...]
    row = lax.broadcasted_iota(jnp.int32, (LANE, CONV_COLS), 0)

    def body(i, carry):
        r0 = pl.multiple_of(i * LANE, LANE)
        cur = pad_ref[pl.ds(CONV_PAD + r0, LANE), :]
        before = pad_ref[pl.ds(r0, CONV_PAD), :]
        after = pad_ref[pl.ds(CONV_PAD + LANE + r0, CONV_PAD), :]
        prev = jnp.where(row == 0, before[CONV_PAD - 1:CONV_PAD, :], pltpu.roll(cur, 1, axis=0))
        nxt = jnp.where(row == LANE - 1, after[0:1, :], pltpu.roll(cur, LANE - 1, axis=0))
        y = prev * w[0:1, :] + cur * w[1:2, :] + nxt * w[2:3, :]
        y = y * jax.nn.sigmoid(y)
        o_ref[pl.ds(r0, LANE), :] = (y * scale).astype(BF16)
        return carry

    lax.fori_loop(0, t // LANE, body, 0)


def _conv_silu(pf, conv_w):
    b, t, _ = pf.shape
    ncol = 2 * C_W // CONV_COLS
    return pl.pallas_call(
        _conv_silu_body,
        grid=(b, ncol),
        in_specs=[
            pl.BlockSpec((None, t, CONV_COLS), lambda i, j: (i, 0, j)),
            pl.BlockSpec((3, CONV_COLS), lambda i, j: (0, j)),
        ],
        out_specs=pl.BlockSpec((None, t, CONV_COLS), lambda i, j: (i, 0, j)),
        out_shape=jax.ShapeDtypeStruct((b, t, 2 * C_W), BF16),
        scratch_shapes=[pltpu.VMEM((t + 2 * CONV_PAD, CONV_COLS), F32)],
        compiler_params=_params(2),
        name="conv_silu",
    )(pf, conv_w)


A_QB = 128
A_KB = A_QB + 2 * HALF_WINDOW


def _dilated_body(slopes_ref, q_ref, k_ref, v_ref, o_ref, *, n_sub, dilation):
    slope = slopes_ref[pl.program_id(2)] * float(dilation)
    row = lax.broadcasted_iota(jnp.int32, (A_QB, A_KB), 0)
    col = lax.broadcasted_iota(jnp.int32, (A_QB, A_KB), 1)
    col_minus_row = col - row

    def body(qb, carry):
        q0 = pl.multiple_of(qb * A_QB, A_QB)
        k0 = pl.multiple_of(jnp.clip(q0 - HALF_WINDOW, 0, n_sub - A_KB), HALF_WINDOW)
        q = q_ref[pl.ds(q0, A_QB), :]
        k = k_ref[pl.ds(k0, A_KB), :]
        v = v_ref[pl.ds(k0, A_KB), :]
        s = lax.dot_general(q, k, (((1,), (1,)), ((), ())), preferred_element_type=F32)
        s = s * (HEAD_DIM ** -0.5)
        delta = jnp.abs(col_minus_row + (k0 - q0))
        s = jnp.where(delta <= HALF_WINDOW, s - slope * delta.astype(F32), NEG)
        m = jnp.max(s, axis=-1, keepdims=True)
        p = jnp.exp(s - m)
        den = jnp.sum(p, axis=-1, keepdims=True)
        o = jnp.dot(p.astype(BF16), v, preferred_element_type=F32) / den
        lse = m + jnp.log(den)
        o_ref[pl.ds(q0, A_QB), 0:HEAD_DIM] = o
        o_ref[pl.ds(q0, A_QB), HEAD_DIM:2 * HEAD_DIM] = jnp.broadcast_to(lse, (A_QB, HEAD_DIM))
        return carry

    lax.fori_loop(0, n_sub // A_QB, body, 0)


def _dilated_attention(p, slopes, dilation):
    b, t, _ = p.shape
    n_sub = t // dilation
    pv = p.reshape(b, n_sub, dilation * P_W)
    pcols = P_W // HEAD_DIM

    def spec(off):
        return pl.BlockSpec((None, n_sub, HEAD_DIM), lambda i, rho, h: (i, 0, rho * pcols + off + h))

    out = pl.pallas_call(
        functools.partial(_dilated_body, n_sub=n_sub, dilation=dilation),
        grid=(b, dilation, A_HEADS),
        in_specs=[
            pl.BlockSpec(memory_space=pltpu.SMEM),
            spec(0), spec(A_HEADS), spec(2 * A_HEADS),
        ],
        out_specs=pl.BlockSpec((None, n_sub, 2 * HEAD_DIM), lambda i, rho, h: (i, 0, rho * A_HEADS + h)),
        out_shape=jax.ShapeDtypeStruct((b, n_sub, dilation * A_HEADS * 2 * HEAD_DIM), F32),
        compiler_params=_params(3),
        name="dilated_attention_r%d" % dilation,
    )(slopes, pv, pv, pv)
    return out.reshape(b, t, A_HEADS * 2 * HEAD_DIM)


B_TQ = 256
B_TK = 512


def _diff_body(slopes_ref, lam_ref, q_ref, k_ref, v_ref, o_ref,
               m1_ref, l1_ref, a1_ref, m2_ref, l2_ref, a2_ref, *, lam_init):
    t = k_ref.shape[0]
    slope = slopes_ref[pl.program_id(1)]
    q0 = pl.program_id(2) * B_TQ
    lane = lax.broadcasted_iota(jnp.int32, (B_TQ, 2 * B_QK_DIM), 1)
    q = q_ref[...] * jnp.asarray(B_QK_DIM ** -0.5, BF16)
    zero = jnp.zeros_like(q)
    q1 = jnp.where(lane < B_QK_DIM, q, zero)
    q2 = jnp.where(lane >= B_QK_DIM, q, zero)
    row = lax.broadcasted_iota(jnp.int32, (B_TQ, B_TK), 0)
    col = lax.broadcasted_iota(jnp.int32, (B_TQ, B_TK), 1)
    col_minus_row = col - row

    for m_ref, l_ref, a_ref in ((m1_ref, l1_ref, a1_ref), (m2_ref, l2_ref, a2_ref)):
        m_ref[...] = jnp.full(m_ref.shape, -jnp.inf, F32)
        l_ref[...] = jnp.zeros(l_ref.shape, F32)
        a_ref[...] = jnp.zeros(a_ref.shape, F32)

    def update(s, v, m_ref, l_ref, a_ref):
        m_old = m_ref[...]
        m_new = jnp.maximum(m_old, jnp.max(s, axis=-1, keepdims=True))
        alpha = jnp.exp(m_old - m_new)
        p = jnp.exp(s - m_new)
        l_ref[...] = alpha * l_ref[...] + jnp.sum(p, axis=-1, keepdims=True)
        a_ref[...] = alpha * a_ref[...] + jnp.dot(p.astype(BF16), v, preferred_element_type=F32)
        m_ref[...] = m_new

    def body(kc, carry):
        k0 = pl.multiple_of(kc * B_TK, B_TK)
        k = k_ref[pl.ds(k0, B_TK), :]
        v = v_ref[pl.ds(k0, B_TK), :]
        bias = (-slope) * jnp.abs(col_minus_row + (k0 - q0)).astype(F32)
        nt = (((1,), (1,)), ((), ()))
        s1 = lax.dot_general(q1, k, nt, preferred_element_type=F32) + bias
        s2 = lax.dot_general(q2, k, nt, preferred_element_type=F32) + bias
        update(s1, v, m1_ref, l1_ref, a1_ref)
        update(s2, v, m2_ref, l2_ref, a2_ref)
        return carry

    lax.fori_loop(0, t // B_TK, body, 0)

    dl = lam_ref[...]
    lam = (jnp.exp(jnp.sum(dl[0:1, :] * dl[1:2, :], axis=-1, keepdims=True))
           - jnp.exp(jnp.sum(dl[2:3, :] * dl[3:4, :], axis=-1, keepdims=True)) + lam_init)
    o_ref[...] = a1_ref[...] / l1_ref[...] - lam * (a2_ref[...] / l2_ref[...])


def _diff_attention(p, slopes, diff_lambda, lam_init):
    b, t, _ = p.shape
    qoff, koff, voff = P_BQ // HEAD_DIM, P_BK // HEAD_DIM, P_BV // HEAD_DIM
    stat = pltpu.VMEM((B_TQ, 1), F32)
    acc = pltpu.VMEM((B_TQ, HEAD_DIM), F32)
    return pl.pallas_call(
        functools.partial(_diff_body, lam_init=lam_init),
        grid=(b, B_HEADS, t // B_TQ),
        in_specs=[
            pl.BlockSpec(memory_space=pltpu.SMEM),
            pl.BlockSpec((4, B_QK_DIM), lambda i, h, qi: (0, 0)),
            pl.BlockSpec((None, B_TQ, HEAD_DIM), lambda i, h, qi: (i, qi, qoff + h)),
            pl.BlockSpec((None, t, HEAD_DIM), lambda i, h, qi: (i, 0, koff + h)),
            pl.BlockSpec((None, t, HEAD_DIM), lambda i, h, qi: (i, 0, voff + h)),
        ],
        out_specs=pl.BlockSpec((None, B_TQ, HEAD_DIM), lambda i, h, qi: (i, qi, h)),
        out_shape=jax.ShapeDtypeStruct((b, t, B_V_W), F32),
        scratch_shapes=[stat, stat, acc, stat, stat, acc],
        compiler_params=_params(3),
        name="diff_attention",
    )(slopes, diff_lambda, p, p, p)


def _log_sigmoid(x):
    return jnp.minimum(x, 0.0) - jnp.log(1.0 + jnp.exp(-jnp.abs(x)))


def _mlstm_body(q_ref, k_ref, v_ref, og_ref, gcol_ref, grow_ref, bcol_ref, brow_ref, o_ref, c_ref):
    lc = MLSTM_CHUNK
    t = q_ref.shape[0]
    nc = t // lc
    row = lax.broadcasted_iota(jnp.int32, (lc, lc), 0)
    col = lax.broadcasted_iota(jnp.int32, (lc, lc), 1)
    lower = col <= row
    upper = col >= row
    tri_lower = lower.astype(F32)
    tri_upper = upper.astype(F32)
    ones_col = (lax.broadcasted_iota(jnp.int32, (lc, LANE), 1) == 0).astype(BF16)
    c_ref[...] = jnp.zeros(c_ref.shape, F32)

    def direction(d, ci, m_state):
        mask, cum_col, cum_row = (lower, tri_lower, tri_upper) if d == 0 else (upper, tri_upper, tri_lower)
        c0 = pl.multiple_of(ci * lc, lc)
        gc = gcol_ref[pl.ds(c0, lc), :] + bcol_ref[...]
        gr = grow_ref[ci] + brow_ref[...]
        li_c = gc[:, 2 * d:2 * d + 1]
        lf_c = _log_sigmoid(gc[:, 2 * d + 1:2 * d + 2])
        li_r = gr[2 * d:2 * d + 1, :]
        lf_r = _log_sigmoid(gr[2 * d + 1:2 * d + 2, :])
        b_c = jnp.sum(cum_col * lf_r, axis=1, keepdims=True)
        b_r = jnp.sum(cum_row * lf_c, axis=0, keepdims=True)
        b_last = jnp.sum(lf_r, axis=-1, keepdims=True)

        q = q_ref[pl.ds(c0, lc), :]
        k = k_ref[pl.ds(c0, lc), :]
        v = v_ref[pl.ds(c0, lc), :]
        c_state = c_ref[d]

        dmat = jnp.where(mask, b_c - b_r + li_r, NEG)
        inter = b_c + m_state
        m_t = jnp.maximum(inter, jnp.max(dmat, axis=-1, keepdims=True))
        dw = jnp.exp(dmat - m_t)
        iw = jnp.exp(inter - m_t)
        sc = lax.dot_general(q, k, (((1,), (1,)), ((), ())), preferred_element_type=F32) * dw
        qc = jnp.dot(q, c_state.astype(BF16), preferred_element_type=F32)
        num = iw * qc[:, 0:HEAD_DIM] + jnp.dot(sc.astype(BF16), v, preferred_element_type=F32)
        den = iw * qc[:, HEAD_DIM:HEAD_DIM + 1] + jnp.sum(sc, axis=-1, keepdims=True)
        h_out = num / jnp.maximum(jnp.abs(den), jnp.exp(-m_t))

        g_c = b_last - b_c + li_c
        g_r = b_last - b_r + li_r
        m_new = jnp.maximum(b_last + m_state, jnp.max(g_r, axis=-1, keepdims=True))
        decay = jnp.exp(b_last + m_state - m_new)
        kg_t = (k.astype(F32) * jnp.exp(g_c - m_new)).T.astype(BF16)
        v_ext = jnp.concatenate([v, ones_col], axis=1)
        c_ref[d] = decay * c_state + jnp.dot(kg_t, v_ext, preferred_element_type=F32)
        return h_out, m_new

    def make_step(first_touch):
        def step(i, carry):
            m_f, m_b = carry
            cf = pl.multiple_of(i * lc, lc)
            cb = pl.multiple_of((nc - 1 - i) * lc, lc)
            h_f, m_f = direction(0, i, m_f)
            h_b, m_b = direction(1, nc - 1 - i, m_b)
            if first_touch:
                o_ref[pl.ds(cf, lc), :] = h_f
                o_ref[pl.ds(cb, lc), :] = h_b
            else:
                for c0, h_new in ((cf, h_f), (cb, h_b)):
                    gate = jax.nn.sigmoid(og_ref[pl.ds(c0, lc), :])
                    o_ref[pl.ds(c0, lc), :] = (o_ref[pl.ds(c0, lc), :] + h_new) * gate
            return m_f, m_b
        return step

    zero = jnp.zeros((1, 1), F32)
    carry = lax.fori_loop(0, nc // 2, make_step(True), (zero, zero))
    lax.fori_loop(nc // 2, nc, make_step(False), carry)


def _mlstm(cqk, p, pf, gate_b):
    b, t, _ = p.shape
    nc = t // MLSTM_CHUNK
    cg = pf[:, :, PF_CG:PF_CG + N_GATES].reshape(b, t, 4, C_HEADS)
    gcol = cg.transpose(0, 3, 1, 2)
    grow = cg.reshape(b, nc, MLSTM_CHUNK, 4, C_HEADS).transpose(0, 4, 1, 3, 2)
    gb = gate_b.reshape(4, C_HEADS).T
    bcol = gb.reshape(C_HEADS, 1, 4)
    brow = gb.reshape(C_HEADS, 4, 1)
    voff, ooff = P_CV // HEAD_DIM, PF_CO // HEAD_DIM
    return pl.pallas_call(
        _mlstm_body,
        grid=(b, C_HEADS),
        in_specs=[
            pl.BlockSpec((None, t, HEAD_DIM), lambda i, h: (i, 0, h)),
            pl.BlockSpec((None, t, HEAD_DIM), lambda i, h: (i, 0, C_HEADS + h)),
            pl.BlockSpec((None, t, HEAD_DIM), lambda i, h: (i, 0, voff + h)),
            pl.BlockSpec((None, t, HEAD_DIM), lambda i, h: (i, 0, ooff + h)),
            pl.BlockSpec((None, None, t, 4), lambda i, h: (i, h, 0, 0)),
            pl.BlockSpec((None, None, nc, 4, MLSTM_CHUNK), lambda i, h: (i, h, 0, 0, 0)),
            pl.BlockSpec((None, 1, 4), lambda i, h: (h, 0, 0)),
            pl.BlockSpec((None, 4, 1), lambda i, h: (h, 0, 0)),
        ],
        out_specs=pl.BlockSpec((None, t, HEAD_DIM), lambda i, h: (i, 0, h)),
        out_shape=jax.ShapeDtypeStruct((b, t, C_W), F32),
        scratch_shapes=[pltpu.VMEM((2, HEAD_DIM, HEAD_DIM + LANE), F32)],
        compiler_params=_params(2),
        name="mlstm",
    )(cqk, cqk, p, pf, gcol, grow, bcol, brow)


MERGE_ROWS = 32


def _merge_out_body(a1_ref, a4_ref, a16_ref, ob_ref, oc_ref, g_ref, w_ref, res_ref, o_ref, lhs_ref, *, b_scale):
    @pl.when(pl.program_id(1) == 0)
    def _():
        def head_norm(o, head):
            ms = jnp.mean(o * o, axis=-1, keepdims=True)
            return o * lax.rsqrt(ms + EPS) * g_ref[:, head * HEAD_DIM:(head + 1) * HEAD_DIM]

        def body(i, carry):
            r0 = pl.multiple_of(i * MERGE_ROWS, MERGE_ROWS)
            rows = pl.ds(r0, MERGE_ROWS)
            for h in range(A_HEADS):
                oc0, lc0 = 2 * h * HEAD_DIM, (2 * h + 1) * HEAD_DIM
                outs = [r[rows, oc0:oc0 + HEAD_DIM] for r in (a1_ref, a4_ref, a16_ref)]
                lses = [r[rows, lc0:lc0 + HEAD_DIM] for r in (a1_ref, a4_ref, a16_ref)]
                mx = jnp.maximum(jnp.maximum(lses[0], lses[1]), lses[2])
                ws = [jnp.exp(l - mx) for l in lses]
                tot = ws[0] + ws[1] + ws[2]
                mixed = (ws[0] / tot) * outs[0] + (ws[1] / tot) * outs[1] + (ws[2] / tot) * outs[2]
                lhs_ref[rows, h * HEAD_DIM:(h + 1) * HEAD_DIM] = head_norm(mixed, h).astype(BF16)
            for h in range(B_HEADS):
                head = A_HEADS + h
                y = head_norm(ob_ref[rows, h * HEAD_DIM:(h + 1) * HEAD_DIM], head) * b_scale
                lhs_ref[rows, head * HEAD_DIM:(head + 1) * HEAD_DIM] = y.astype(BF16)
            for h in range(C_HEADS):
                head = A_HEADS + B_HEADS + h
                y = head_norm(oc_ref[rows, h * HEAD_DIM:(h + 1) * HEAD_DIM], head)
                lhs_ref[rows, head * HEAD_DIM:(head + 1) * HEAD_DIM] = y.astype(BF16)
            return carry

        lax.fori_loop(0, lhs_ref.shape[0] // MERGE_ROWS, body, 0)

    o_ref[...] = res_ref[...] + jnp.dot(lhs_ref[...], w_ref[...], preferred_element_type=F32)


def _merge_out(oa, ob, oc, g, w, res, b_scale, tm, tn):
    m, n = res.shape
    aw = A_HEADS * 2 * HEAD_DIM
    row = lambda width: pl.BlockSpec((tm, width), lambda i, j: (i, 0))
    return pl.pallas_call(
        functools.partial(_merge_out_body, b_scale=b_scale),
        grid=(m // tm, n // tn),
        in_specs=[
            row(aw), row(aw), row(aw), row(B_V_W), row(C_W),
            pl.BlockSpec((1, MIX_W), lambda i, j: (0, 0)),
            pl.BlockSpec((MIX_W, tn), lambda i, j: (0, j)),
            pl.BlockSpec((tm, tn), lambda i, j: (i, j)),
        ],
        out_specs=pl.BlockSpec((tm, tn), lambda i, j: (i, j)),
        out_shape=jax.ShapeDtypeStruct((m, n), F32),
        scratch_shapes=[pltpu.VMEM((tm, MIX_W), BF16)],
        compiler_params=_params(2),
        name="merge_out_proj",
    )(oa[0].reshape(m, aw), oa[1].reshape(m, aw), oa[2].reshape(m, aw),
      ob.reshape(m, B_V_W), oc.reshape(m, C_W), g.reshape(1, MIX_W), w, res)


def _xattn_out_body(q_ref, k_ref, v_ref, w_ref, res_ref, o_ref, lhs_ref):
    @pl.when(pl.program_id(1) == 0)
    def _():
        for h in range(X_HEADS):
            cols = slice(h * X_HEAD_DIM, (h + 1) * X_HEAD_DIM)
            s = lax.dot_general(q_ref[:, cols], k_ref[:, cols], (((1,), (1,)), ((), ())),
                                preferred_element_type=F32) * (X_HEAD_DIM ** -0.5)
            m = jnp.max(s, axis=-1, keepdims=True)
            p = jnp.exp(s - m)
            a = p / jnp.sum(p, axis=-1, keepdims=True)
            lhs_ref[:, cols] = jnp.dot(a.astype(BF16), v_ref[:, cols], preferred_element_type=F32).astype(BF16)

    o_ref[...] = res_ref[...] + jnp.dot(lhs_ref[...], w_ref[...], preferred_element_type=F32)


def _xattn_out(q, kv, w, res, seq, tm, tn):
    m, n = res.shape
    mem_len = kv.shape[0] // (m // seq)
    per_batch = seq // tm
    return pl.pallas_call(
        _xattn_out_body,
        grid=(m // tm, n // tn),
        in_specs=[
            pl.BlockSpec((tm, D_MODEL), lambda i, j: (i, 0)),
            pl.BlockSpec((mem_len, D_MODEL), lambda i, j: (i // per_batch, 0)),
            pl.BlockSpec((mem_len, D_MODEL), lambda i, j: (i // per_batch, 1)),
            pl.BlockSpec((D_MODEL, tn), lambda i, j: (0, j)),
            pl.BlockSpec((tm, tn), lambda i, j: (i, j)),
        ],
        out_specs=pl.BlockSpec((tm, tn), lambda i, j: (i, j)),
        out_shape=jax.ShapeDtypeStruct((m, n), F32),
        scratch_shapes=[pltpu.VMEM((tm, D_MODEL), BF16)],
        compiler_params=_params(2),
        name="xattn_out_proj",
    )(q, kv, kv, w, res)


def _mlp_body(x_ref, g_ref, wu_ref, wd_ref, fg_ref, o_ref, xn_ref, *, final_norm):
    f = pl.program_id(1)

    @pl.when(f == 0)
    def _():
        _rmsnorm_rows_to(x_ref, g_ref, xn_ref)

        def copy(i, carry):
            rows = pl.ds(pl.multiple_of(i * ROW_CHUNK, ROW_CHUNK), ROW_CHUNK)
            o_ref[rows, :] = x_ref[rows, :]
            return carry

        lax.fori_loop(0, x_ref.shape[0] // ROW_CHUNK, copy, 0)

    u = jnp.dot(xn_ref[...], wu_ref[...], preferred_element_type=F32)
    a = jnp.square(jnp.maximum(u, 0.0)).astype(BF16)
    o_ref[...] += jnp.dot(a, wd_ref[...], preferred_element_type=F32)

    if final_norm:
        @pl.when(f == pl.num_programs(1) - 1)
        def _():
            def norm(i, carry):
                rows = pl.ds(pl.multiple_of(i * ROW_CHUNK, ROW_CHUNK), ROW_CHUNK)
                y = o_ref[rows, :]
                ms = jnp.mean(y * y, axis=-1, keepdims=True)
                o_ref[rows, :] = y * lax.rsqrt(ms + EPS) * fg_ref[...]
                return carry

            lax.fori_loop(0, o_ref.shape[0] // ROW_CHUNK, norm, 0)


def _mlp(x, g, w_up, w_down, final_g, final_norm, tm, tf):
    m, d = x.shape
    d_ff = w_up.shape[1]
    return pl.pallas_call(
        functools.partial(_mlp_body, final_norm=final_norm),
        grid=(m // tm, d_ff // tf),
        in_specs=[
            pl.BlockSpec((tm, d), lambda i, f: (i, 0)),
            pl.BlockSpec((1, d), lambda i, f: (0, 0)),
            pl.BlockSpec((d, tf), lambda i, f: (0, f)),
            pl.BlockSpec((tf, d), lambda i, f: (f, 0)),
            pl.BlockSpec((1, d), lambda i, f: (0, 0)),
        ],
        out_specs=pl.BlockSpec((tm, d), lambda i, f: (i, 0)),
        out_shape=jax.ShapeDtypeStruct((m, d), F32),
        scratch_shapes=[pltpu.VMEM((tm, d), BF16)],
        compiler_params=_params(2),
        name="mlp",
    )(x, g.reshape(1, d), w_up, w_down, final_g.reshape(1, d))


def _split_in_proj(w_in):
    o = np.cumsum((0, A_W, A_W, A_W, B_QK_W, B_QK_W, B_V_W, 2 * C_W, C_W, C_W, N_GATES))
    wp = jnp.concatenate([w_in[:, o[0]:o[6]], w_in[:, o[7]:o[8]]], axis=1)
    pad = jnp.zeros((w_in.shape[0], HEAD_DIM - N_GATES), w_in.dtype)
    wpf = jnp.concatenate([w_in[:, o[6]:o[7]], w_in[:, o[8]:o[10]], pad], axis=1)
    return wp.astype(BF16), wpf.astype(BF16)


def kernel(x, mem, norm_mix_g, w_in, conv_w, gate_b, diff_lambda, head_norm_g, w_out, norm_x_g, norm_mem_g,
           w_xq, w_xkv, w_xo, norm_mlp_g, w_up, w_down, final_norm_g):
    b, t, d = x.shape
    m = b * t
    depth = w_in.shape[0]
    h = x.reshape(m, d)
    mem2 = mem.reshape(-1, d)
    slopes_a = _alibi_slopes(A_HEADS)
    slopes_b = _alibi_slopes(B_HEADS)
    for layer in range(depth):
        lam_init = 0.8 - 0.6 * math.exp(-0.3 * layer)
        wp, wpf = _split_in_proj(w_in[layer])
        p = _norm_matmul(h, norm_mix_g[layer], wp, BF16, 1024, 768).reshape(b, t, P_W)
        pf = _norm_matmul(h, norm_mix_g[layer], wpf, F32, 512, PF_W).reshape(b, t, PF_W)
        cqk = _conv_silu(pf, conv_w[layer])
        oa = [_dilated_attention(p, slopes_a, r) for _, r in DILATED_CONFIGS]
        ob = _diff_attention(p, slopes_b, diff_lambda[layer], lam_init)
        oc = _mlstm(cqk, p, pf, gate_b[layer])
        h = _merge_out(oa, ob, oc, head_norm_g[layer], w_out[layer].astype(BF16), h, 1.0 - lam_init, 512, 1024)

        q = _norm_matmul(h, norm_x_g[layer], w_xq[layer].astype(BF16), BF16, 1024, 1024)
        kv = _norm_matmul(mem2, norm_mem_g[layer], w_xkv[layer].astype(BF16), BF16, 512, 1024)
        h = _xattn_out(q, kv, w_xo[layer].astype(BF16), h, t, 512, 1024)

        h = _mlp(h, norm_mlp_g[layer], w_up[layer].astype(BF16), w_down[layer].astype(BF16), final_norm_g,
                 layer == depth - 1, 1024, 512)
    return h.reshape(b, t, d)
```

```python
import functools
import math

import numpy as np
import jax
import jax.numpy as jnp
from jax import lax
from jax.experimental import pallas as pl
from jax.experimental.pallas import tpu as pltpu

F32 = jnp.float32
BF16 = jnp.bfloat16

D_MODEL = 2048
HEAD_DIM = 128
A_HEADS = 6
B_HEADS = 4
C_HEADS = 6
B_QK_DIM = 64
DILATED_CONFIGS = ((128, 1), (512, 4), (2048, 16))
HALF_WINDOW = 64
MLSTM_CHUNK = 128
X_HEADS = 4
X_HEAD_DIM = D_MODEL // X_HEADS
EPS = 1e-6
NEG = -1e30

A_W = A_HEADS * HEAD_DIM
B_QK_W = B_HEADS * 2 * B_QK_DIM
B_V_W = B_HEADS * HEAD_DIM
C_W = C_HEADS * HEAD_DIM
N_GATES = 4 * C_HEADS
N_MIX_HEADS = A_HEADS + B_HEADS + C_HEADS
MIX_W = N_MIX_HEADS * HEAD_DIM

P_W = 3 * A_W + 2 * B_QK_W + B_V_W + C_W
P_BQ = 3 * A_W
P_BK = P_BQ + B_QK_W
P_BV = P_BK + B_QK_W
P_CV = P_BV + B_V_W
PF_CO = 2 * C_W
PF_CG = PF_CO + C_W
PF_W = PF_CG + HEAD_DIM

LOG2E = math.log2(math.e)
A_Q_SCALE = HEAD_DIM ** -0.5 * LOG2E
B_Q_SCALE = B_QK_DIM ** -0.5 * LOG2E

LANE = 128
ROW_CHUNK = 16
VMEM_LIMIT = 56 * 1024 * 1024


def _params(n_axes, vmem=VMEM_LIMIT):
    return pltpu.CompilerParams(dimension_semantics=("arbitrary",) * n_axes, vmem_limit_bytes=vmem)


def _alibi_slopes(n_heads):
    return jnp.asarray(2.0 ** (-8.0 * np.arange(1, n_heads + 1) / n_heads), dtype=F32)


def _rmsnorm_rows_to(x_ref, g_ref, dst_ref):
    rows = x_ref.shape[0]

    def body(i, carry):
        r0 = pl.multiple_of(i * ROW_CHUNK, ROW_CHUNK)
        x = x_ref[pl.ds(r0, ROW_CHUNK), :]
        ms = jnp.mean(x * x, axis=-1, keepdims=True)
        dst_ref[pl.ds(r0, ROW_CHUNK), :] = (x * lax.rsqrt(ms + EPS) * g_ref[...]).astype(BF16)
        return carry

    lax.fori_loop(0, rows // ROW_CHUNK, body, 0)


def _norm_matmul_body(x_ref, g_ref, w_ref, *rest, scaled):
    s_ref = rest[0] if scaled else None
    o_ref, xn_ref = rest[-2:]

    @pl.when(pl.program_id(1) == 0)
    def _():
        _rmsnorm_rows_to(x_ref, g_ref, xn_ref)

    acc = jnp.dot(xn_ref[...], w_ref[...], preferred_element_type=F32)
    if scaled:
        acc = acc * s_ref[...]
    o_ref[...] = acc.astype(o_ref.dtype)


def _norm_matmul(x, g, w, out_dtype, tm, tn, col_scale=None):
    m, k = x.shape
    n = w.shape[1]
    scaled = col_scale is not None
    in_specs = [
        pl.BlockSpec((tm, k), lambda i, j: (i, 0)),
        pl.BlockSpec((1, k), lambda i, j: (0, 0)),
        pl.BlockSpec((k, tn), lambda i, j: (0, j)),
    ]
    args = [x, g.reshape(1, k), w]
    if scaled:
        in_specs.append(pl.BlockSpec((1, tn), lambda i, j: (0, j)))
        args.append(col_scale.reshape(1, n))
    return pl.pallas_call(
        functools.partial(_norm_matmul_body, scaled=scaled),
        grid=(m // tm, n // tn),
        in_specs=in_specs,
        out_specs=pl.BlockSpec((tm, tn), lambda i, j: (i, j)),
        out_shape=jax.ShapeDtypeStruct((m, n), out_dtype),
        scratch_shapes=[pltpu.VMEM((tm, k), BF16)],
        compiler_params=_params(2),
        name="norm_matmul",
    )(*args)


CONV_COLS = 256
CONV_PAD = 8


def _conv_silu_body(x_ref, w_ref, o_ref, pad_ref):
    t = x_ref.shape[0]
    zeros = jnp.zeros((CONV_PAD, CONV_COLS), F32)
    pad_ref[0:CONV_PAD, :] = zeros
    pad_ref[CONV_PAD + t:2 * CONV_PAD + t, :] = zeros
    scale = jnp.where(pl.program_id(1) < C_W // CONV_COLS, HEAD_DIM ** -0.5, 1.0).astype(F32)

    def copy(i, carry):
        r0 = pl.multiple_of(i * LANE, LANE)
        pad_ref[pl.ds(CONV_PAD + r0, LANE), :] = x_ref[pl.ds(r0, LANE), :]
        return carry

    lax.fori_loop(0, t // LANE, copy, 0)
    w = w_ref[...]
    row = lax.broadcasted_iota(jnp.int32, (LANE, CONV_COLS), 0)

    def body(i, carry):
        r0 = pl.multiple_of(i * LANE, LANE)
        cur = pad_ref[pl.ds(CONV_PAD + r0, LANE), :]
        before = pad_ref[pl.ds(r0, CONV_PAD), :]
        after = pad_ref[pl.ds(CONV_PAD + LANE + r0, CONV_PAD), :]
        prev = jnp.where(row == 0, before[CONV_PAD - 1:CONV_PAD, :], pltpu.roll(cur, 1, axis=0))
        nxt = jnp.where(row == LANE - 1, after[0:1, :], pltpu.roll(cur, LANE - 1, axis=0))
        y = prev * w[0:1, :] + cur * w[1:2, :] + nxt * w[2:3, :]
        y = y * jax.nn.sigmoid(y)
        o_ref[pl.ds(r0, LANE), :] = (y * scale).astype(BF16)
        return carry

    lax.fori_loop(0, t // LANE, body, 0)


def _conv_silu(pf, conv_w):
    b, t, _ = pf.shape
    ncol = 2 * C_W // CONV_COLS
    return pl.pallas_call(
        _conv_silu_body,
        grid=(b, ncol),
        in_specs=[
            pl.BlockSpec((None, t, CONV_COLS), lambda i, j: (i, 0, j)),
            pl.BlockSpec((3, CONV_COLS), lambda i, j: (0, j)),
        ],
        out_specs=pl.BlockSpec((None, t, CONV_COLS), lambda i, j: (i, 0, j)),
        out_shape=jax.ShapeDtypeStruct((b, t, 2 * C_W), BF16),
        scratch_shapes=[pltpu.VMEM((t + 2 * CONV_PAD, CONV_COLS), F32)],
        compiler_params=_params(2),
        name="conv_silu",
    )(pf, conv_w)


A_QB = 128
A_KB = A_QB + 2 * HALF_WINDOW


def _dilated_mix_body(slopes_ref, q_ref, k_ref, v_ref, o_ref,
                      qf_ref, kf_ref, vf_ref, m_ref, l_ref, acc_ref, bias_ref):
    t = q_ref.shape[0]
    n_blocks = t // A_QB
    base_slope = slopes_ref[pl.program_id(1)] * LOG2E
    row = lax.broadcasted_iota(jnp.int32, (A_QB, A_KB), 0)
    col = lax.broadcasted_iota(jnp.int32, (A_QB, A_KB), 1)
    n_cfg = len(DILATED_CONFIGS)

    for g, (_, r) in enumerate(DILATED_CONFIGS):
        for variant in range(3):
            delta = jnp.abs(col - row - variant * HALF_WINDOW)
            bias_ref[g * 3 + variant] = jnp.where(delta <= HALF_WINDOW, (-base_slope * float(r)) * delta.astype(F32), NEG)

    def widen(i, carry):
        rows = pl.ds(pl.multiple_of(i * A_QB, A_QB), A_QB)
        qf_ref[rows, :] = q_ref[rows, :].astype(F32)
        kf_ref[rows, :] = k_ref[rows, :].astype(F32)
        vf_ref[rows, :] = v_ref[rows, :].astype(F32)
        return carry

    lax.fori_loop(0, n_blocks, widen, 0)

    for g, (_, r) in enumerate(DILATED_CONFIGS):
        n_sub = t // r
        nqb = n_sub // A_QB

        def rows_of(bi, r=r, n_sub=n_sub, nqb=nqb):
            rho = bi // nqb
            q0 = (bi % nqb) * A_QB
            k0 = jnp.clip(q0 - HALF_WINDOW, 0, n_sub - A_KB)
            if r == 1:
                qrows = pl.ds(pl.multiple_of(q0, A_QB), A_QB)
                krows = pl.ds(pl.multiple_of(k0, HALF_WINDOW), A_KB)
            else:
                qrows = pl.ds(rho + r * q0, A_QB, stride=r)
                krows = pl.ds(rho + r * k0, A_KB, stride=r)
            return qrows, krows, (q0 - k0) // HALF_WINDOW

        def scores(bi, g=g, rows_of=rows_of):
            qrows, krows, variant = rows_of(jnp.minimum(bi, n_blocks - 1))
            q = qf_ref[qrows, :].astype(BF16)
            k = kf_ref[krows, :].astype(BF16)
            s = lax.dot_general(q, k, (((1,), (1,)), ((), ())), preferred_element_type=F32)
            return s + bias_ref[g * 3 + variant]

        def block(bi, s, g=g, rows_of=rows_of, scores=scores):
            s_next = scores(bi + 1)
            qrows, krows, _ = rows_of(bi)
            v = vf_ref[krows, :].astype(BF16)
            m = jnp.max(s, axis=-1, keepdims=True)
            p = jnp.exp2(s - m)
            den = jnp.sum(p, axis=-1, keepdims=True)
            o = jnp.dot(p.astype(BF16), v, preferred_element_type=F32) / den
            lse = jnp.broadcast_to(m + jnp.log2(den), (A_QB, HEAD_DIM))
            if g == 0:
                m_ref[qrows, :] = lse
                l_ref[qrows, :] = jnp.ones((A_QB, HEAD_DIM), F32)
                acc_ref[qrows, :] = o
            else:
                m_old = m_ref[qrows, :]
                m_new = jnp.maximum(m_old, lse)
                a = jnp.exp2(m_old - m_new)
                w = jnp.exp2(lse - m_new)
                l_new = a * l_ref[qrows, :] + w
                acc_new = a * acc_ref[qrows, :] + w * o
                if g == n_cfg - 1:
                    o_ref[qrows, :] = acc_new / l_new
                else:
                    m_ref[qrows, :] = m_new
                    l_ref[qrows, :] = l_new
                    acc_ref[qrows, :] = acc_new
            return s_next

        lax.fori_loop(0, n_blocks, block, scores(0), unroll=4)


def _dilated_mixture(p, slopes):
    b, t, _ = p.shape
    spec = lambda off: pl.BlockSpec((None, t, HEAD_DIM), lambda i, h: (i, 0, off + h))
    seq = lambda: pltpu.VMEM((t, HEAD_DIM), F32)
    return pl.pallas_call(
        _dilated_mix_body,
        grid=(b, A_HEADS),
        in_specs=[pl.BlockSpec(memory_space=pltpu.SMEM), spec(0), spec(A_HEADS), spec(2 * A_HEADS)],
        out_specs=pl.BlockSpec((None, t, HEAD_DIM), lambda i, h: (i, 0, h)),
        out_shape=jax.ShapeDtypeStruct((b, t, A_W), F32),
        scratch_shapes=[seq(), seq(), seq(), seq(), seq(), seq(),
                        pltpu.VMEM((3 * len(DILATED_CONFIGS), A_QB, A_KB), F32)],
        compiler_params=_params(2),
        name="dilated_mixture",
    )(slopes, p, p, p)


B_BLK = 256


def _diff_bias_coefs():
    c = (2.0 ** (-8.0 * np.arange(1, B_HEADS + 1) / B_HEADS)).astype(np.float64) * LOG2E
    out = np.zeros((B_HEADS, 4), np.float32)
    rest = c.copy()
    for i in range(3):
        term = rest.astype(np.float32).astype(BF16).astype(np.float64)
        out[:, i] = term
        rest = rest - term
    out[:, 3] = c
    return out


def _diff_body(coef_ref, lam_ref, q_ref, k_ref, v_ref, o_ref, kaug_ref, vt_ref, *, lam_init):
    h = pl.program_id(1)
    qi = pl.program_id(2)
    nkb = k_ref.shape[0] // B_BLK
    c1, c2, c3, c = coef_ref[h, 0], coef_ref[h, 1], coef_ref[h, 2], coef_ref[h, 3]
    lane = lax.broadcasted_iota(jnp.int32, (B_BLK, LANE), 1)
    pos = lax.broadcasted_iota(jnp.int32, (B_BLK, LANE), 0).astype(F32)
    coefs = jnp.where(lane % 3 == 0, c1, jnp.where(lane % 3 == 1, c2, c3))

    @pl.when(qi == 0)
    def _():
        feat = jnp.where(lane < 3, -pos, jnp.where(lane < 6, coefs, 0.0)).astype(BF16)

        def fill(i, carry):
            rows = pl.ds(pl.multiple_of(i * B_BLK, B_BLK), B_BLK)
            kaug_ref[i, :, 0:LANE] = k_ref[rows, :]
            kaug_ref[i, :, LANE:2 * LANE] = feat
            vt_ref[i] = v_ref[rows, :].astype(F32).T.astype(BF16)
            return carry

        lax.fori_loop(0, nkb, fill, 0)

    qfeat = jnp.where(lane < 3, coefs, jnp.where(lane < 6, pos, 0.0))
    qfeat_right = qfeat.astype(BF16)
    qfeat_left = (-qfeat).astype(BF16)
    q = q_ref[...]
    zero = jnp.zeros_like(q)
    q1 = jnp.where(lane < B_QK_DIM, q, zero)
    q2 = jnp.where(lane >= B_QK_DIM, q, zero)
    nt = (((1,), (1,)), ((), ()))

    def scores(kb, qf):
        q_aug = jnp.concatenate([jnp.concatenate([q1, qf], axis=1), jnp.concatenate([q2, qf], axis=1)], axis=0)
        return lax.dot_general(kaug_ref[kb], q_aug, nt, preferred_element_type=F32)

    s = jnp.minimum(scores(qi, qfeat_right), scores(qi, qfeat_left))
    m_run = jnp.max(s, axis=0, keepdims=True)
    p = jnp.exp2(s - m_run)
    den = jnp.sum(p, axis=0, keepdims=True)
    acc = jnp.dot(vt_ref[qi], p.astype(BF16), preferred_element_type=F32)

    def off_diagonal(n):
        kb = n + (n >= qi).astype(jnp.int32)
        return kb, scores(kb, jnp.where(n < qi, qfeat_left, qfeat_right))

    nxt = off_diagonal(0)
    for n in range(nkb - 1):
        kb, s = nxt
        if n + 1 < nkb - 1:
            nxt = off_diagonal(n + 1)
        shift = (-c * B_BLK) * jnp.abs(kb - qi).astype(F32)
        m_new = jnp.maximum(m_run, jnp.max(s, axis=0, keepdims=True) + shift)
        alpha = jnp.exp2(m_run - m_new)
        p = jnp.exp2(s - (m_new - shift))
        den = alpha * den + jnp.sum(p, axis=0, keepdims=True)
        acc = alpha * acc + jnp.dot(vt_ref[kb], p.astype(BF16), preferred_element_type=F32)
        m_run = m_new

    dl = lam_ref[...]
    lam = (jnp.exp(jnp.sum(dl[0:1, :] * dl[1:2, :], axis=-1, keepdims=True))
           - jnp.exp(jnp.sum(dl[2:3, :] * dl[3:4, :], axis=-1, keepdims=True)) + lam_init)
    o = acc / den
    o_ref[...] = (o[:, 0:B_BLK] - lam * o[:, B_BLK:2 * B_BLK]).T


def _diff_attention(p, diff_lambda, lam_init):
    b, t, _ = p.shape
    nkb = t // B_BLK
    qoff, koff, voff = P_BQ // HEAD_DIM, P_BK // HEAD_DIM, P_BV // HEAD_DIM
    return pl.pallas_call(
        functools.partial(_diff_body, lam_init=lam_init),
        grid=(b, B_HEADS, nkb),
        in_specs=[
            pl.BlockSpec(memory_space=pltpu.SMEM),
            pl.BlockSpec((4, B_QK_DIM), lambda i, h, qi: (0, 0)),
            pl.BlockSpec((None, B_BLK, HEAD_DIM), lambda i, h, qi: (i, qi, qoff + h)),
            pl.BlockSpec((None, t, HEAD_DIM), lambda i, h, qi: (i, 0, koff + h)),
            pl.BlockSpec((None, t, HEAD_DIM), lambda i, h, qi: (i, 0, voff + h)),
        ],
        out_specs=pl.BlockSpec((None, B_BLK, HEAD_DIM), lambda i, h, qi: (i, qi, h)),
        out_shape=jax.ShapeDtypeStruct((b, t, B_V_W), F32),
        scratch_shapes=[pltpu.VMEM((nkb, B_BLK, 2 * LANE), BF16), pltpu.VMEM((nkb, HEAD_DIM, B_BLK), BF16)],
        compiler_params=_params(3),
        name="diff_attention",
    )(jnp.asarray(_diff_bias_coefs()), diff_lambda, p, p, p)


def _log_sigmoid(x):
    return jnp.minimum(x, 0.0) - jnp.log(1.0 + jnp.exp(-jnp.abs(x)))


def _mlstm_body(q_ref, k_ref, v_ref, og_ref, gcol_ref, grow_ref, bcol_ref, brow_ref, o_ref, c_ref):
    lc = MLSTM_CHUNK
    t = q_ref.shape[0]
    nc = t // lc
    row = lax.broadcasted_iota(jnp.int32, (lc, lc), 0)
    col = lax.broadcasted_iota(jnp.int32, (lc, lc), 1)
    lower = col <= row
    upper = col >= row
    tri_lower = lower.astype(F32)
    tri_upper = upper.astype(F32)
    ones_col = (lax.broadcasted_iota(jnp.int32, (lc, LANE), 1) == 0).astype(BF16)
    c_ref[...] = jnp.zeros(c_ref.shape, F32)

    def direction(d, ci, m_state):
        mask, cum_col, cum_row = (lower, tri_lower, tri_upper) if d == 0 else (upper, tri_upper, tri_lower)
        c0 = pl.multiple_of(ci * lc, lc)
        gc = gcol_ref[pl.ds(c0, lc), :] + bcol_ref[...]
        gr = grow_ref[ci] + brow_ref[...]
        li_c = gc[:, 2 * d:2 * d + 1]
        lf_c = _log_sigmoid(gc[:, 2 * d + 1:2 * d + 2])
        li_r = gr[2 * d:2 * d + 1, :]
        lf_r = _log_sigmoid(gr[2 * d + 1:2 * d + 2, :])
        b_c = jnp.sum(cum_col * lf_r, axis=1, keepdims=True)
        b_r = jnp.sum(cum_row * lf_c, axis=0, keepdims=True)
        b_last = jnp.sum(lf_r, axis=-1, keepdims=True)

        q = q_ref[pl.ds(c0, lc), :]
        k = k_ref[pl.ds(c0, lc), :]
        v = v_ref[pl.ds(c0, lc), :]
        c_state = c_ref[d]

        dmat = jnp.where(mask, b_c - b_r + li_r, NEG)
        inter = b_c + m_state
        m_t = jnp.maximum(inter, jnp.max(dmat, axis=-1, keepdims=True))
        dw = jnp.exp(dmat - m_t)
        iw = jnp.exp(inter - m_t)
        sc = lax.dot_general(q, k, (((1,), (1,)), ((), ())), preferred_element_type=F32) * dw
        qc = jnp.dot(q, c_state.astype(BF16), preferred_element_type=F32)
        num = iw * qc[:, 0:HEAD_DIM] + jnp.dot(sc.astype(BF16), v, preferred_element_type=F32)
        den = iw * qc[:, HEAD_DIM:HEAD_DIM + 1] + jnp.sum(sc, axis=-1, keepdims=True)
        h_out = num / jnp.maximum(jnp.abs(den), jnp.exp(-m_t))

        g_c = b_last - b_c + li_c
        g_r = b_last - b_r + li_r
        m_new = jnp.maximum(b_last + m_state, jnp.max(g_r, axis=-1, keepdims=True))
        decay = jnp.exp(b_last + m_state - m_new)
        kg_t = (k.astype(F32) * jnp.exp(g_c - m_new)).T.astype(BF16)
        v_ext = jnp.concatenate([v, ones_col], axis=1)
        c_ref[d] = decay * c_state + jnp.dot(kg_t, v_ext, preferred_element_type=F32)
        return h_out, m_new

    def make_step(first_touch):
        def step(i, carry):
            m_f, m_b = carry
            cf = pl.multiple_of(i * lc, lc)
            cb = pl.multiple_of((nc - 1 - i) * lc, lc)
            h_f, m_f = direction(0, i, m_f)
            h_b, m_b = direction(1, nc - 1 - i, m_b)
            if first_touch:
                o_ref[pl.ds(cf, lc), :] = h_f
                o_ref[pl.ds(cb, lc), :] = h_b
            else:
                for c0, h_new in ((cf, h_f), (cb, h_b)):
                    gate = jax.nn.sigmoid(og_ref[pl.ds(c0, lc), :])
                    o_ref[pl.ds(c0, lc), :] = (o_ref[pl.ds(c0, lc), :] + h_new) * gate
            return m_f, m_b
        return step

    zero = jnp.zeros((1, 1), F32)
    carry = lax.fori_loop(0, nc // 2, make_step(True), (zero, zero))
    lax.fori_loop(nc // 2, nc, make_step(False), carry)


def _mlstm(cqk, p, pf, gate_b):
    b, t, _ = p.shape
    nc = t // MLSTM_CHUNK
    cg = pf[:, :, PF_CG:PF_CG + N_GATES].reshape(b, t, 4, C_HEADS)
    gcol = cg.transpose(0, 3, 1, 2)
    grow = cg.reshape(b, nc, MLSTM_CHUNK, 4, C_HEADS).transpose(0, 4, 1, 3, 2)
    gb = gate_b.reshape(4, C_HEADS).T
    bcol = gb.reshape(C_HEADS, 1, 4)
    brow = gb.reshape(C_HEADS, 4, 1)
    voff, ooff = P_CV // HEAD_DIM, PF_CO // HEAD_DIM
    return pl.pallas_call(
        _mlstm_body,
        grid=(b, C_HEADS),
        in_specs=[
            pl.BlockSpec((None, t, HEAD_DIM), lambda i, h: (i, 0, h)),
            pl.BlockSpec((None, t, HEAD_DIM), lambda i, h: (i, 0, C_HEADS + h)),
            pl.BlockSpec((None, t, HEAD_DIM), lambda i, h: (i, 0, voff + h)),
            pl.BlockSpec((None, t, HEAD_DIM), lambda i, h: (i, 0, ooff + h)),
            pl.BlockSpec((None, None, t, 4), lambda i, h: (i, h, 0, 0)),
            pl.BlockSpec((None, None, nc, 4, MLSTM_CHUNK), lambda i, h: (i, h, 0, 0, 0)),
            pl.BlockSpec((None, 1, 4), lambda i, h: (h, 0, 0)),
            pl.BlockSpec((None, 4, 1), lambda i, h: (h, 0, 0)),
        ],
        out_specs=pl.BlockSpec((None, t, HEAD_DIM), lambda i, h: (i, 0, h)),
        out_shape=jax.ShapeDtypeStruct((b, t, C_W), F32),
        scratch_shapes=[pltpu.VMEM((2, HEAD_DIM, HEAD_DIM + LANE), F32)],
        compiler_params=_params(2),
        name="mlstm",
    )(cqk, cqk, p, pf, gcol, grow, bcol, brow)


MERGE_ROWS = 32


def _merge_out_body(oa_ref, ob_ref, oc_ref, g_ref, w_ref, res_ref, o_ref, lhs_ref, *, b_scale):
    @pl.when(pl.program_id(1) == 0)
    def _():
        def body(i, carry):
            rows = pl.ds(pl.multiple_of(i * MERGE_ROWS, MERGE_ROWS), MERGE_ROWS)
            head = 0
            for src_ref, n_heads, scale in ((oa_ref, A_HEADS, None), (ob_ref, B_HEADS, b_scale), (oc_ref, C_HEADS, None)):
                for h in range(n_heads):
                    o = src_ref[rows, h * HEAD_DIM:(h + 1) * HEAD_DIM]
                    ms = jnp.mean(o * o, axis=-1, keepdims=True)
                    y = o * lax.rsqrt(ms + EPS) * g_ref[:, head * HEAD_DIM:(head + 1) * HEAD_DIM]
                    if scale is not None:
                        y = y * scale
                    lhs_ref[rows, head * HEAD_DIM:(head + 1) * HEAD_DIM] = y.astype(BF16)
                    head += 1
            return carry

        lax.fori_loop(0, lhs_ref.shape[0] // MERGE_ROWS, body, 0)

    o_ref[...] = res_ref[...] + jnp.dot(lhs_ref[...], w_ref[...], preferred_element_type=F32)


def _merge_out(oa, ob, oc, g, w, res, b_scale, tm, tn):
    m, n = res.shape
    row = lambda width: pl.BlockSpec((tm, width), lambda i, j: (i, 0))
    return pl.pallas_call(
        functools.partial(_merge_out_body, b_scale=b_scale),
        grid=(m // tm, n // tn),
        in_specs=[
            row(A_W), row(B_V_W), row(C_W),
            pl.BlockSpec((1, MIX_W), lambda i, j: (0, 0)),
            pl.BlockSpec((MIX_W, tn), lambda i, j: (0, j)),
            pl.BlockSpec((tm, tn), lambda i, j: (i, j)),
        ],
        out_specs=pl.BlockSpec((tm, tn), lambda i, j: (i, j)),
        out_shape=jax.ShapeDtypeStruct((m, n), F32),
        scratch_shapes=[pltpu.VMEM((tm, MIX_W), BF16)],
        compiler_params=_params(2),
        name="merge_out_proj",
    )(oa.reshape(m, A_W), ob.reshape(m, B_V_W), oc.reshape(m, C_W), g.reshape(1, MIX_W), w, res)


def _xattn_out_body(q_ref, k_ref, v_ref, w_ref, res_ref, o_ref, lhs_ref):
    @pl.when(pl.program_id(1) == 0)
    def _():
        for h in range(X_HEADS):
            cols = slice(h * X_HEAD_DIM, (h + 1) * X_HEAD_DIM)
            s = lax.dot_general(q_ref[:, cols], k_ref[:, cols], (((1,), (1,)), ((), ())),
                                preferred_element_type=F32) * (X_HEAD_DIM ** -0.5)
            m = jnp.max(s, axis=-1, keepdims=True)
            p = jnp.exp(s - m)
            a = p / jnp.sum(p, axis=-1, keepdims=True)
            lhs_ref[:, cols] = jnp.dot(a.astype(BF16), v_ref[:, cols], preferred_element_type=F32).astype(BF16)

    o_ref[...] = res_ref[...] + jnp.dot(lhs_ref[...], w_ref[...], preferred_element_type=F32)


def _xattn_out(q, kv, w, res, seq, tm, tn):
    m, n = res.shape
    mem_len = kv.shape[0] // (m // seq)
    per_batch = seq // tm
    return pl.pallas_call(
        _xattn_out_body,
        grid=(m // tm, n // tn),
        in_specs=[
            pl.BlockSpec((tm, D_MODEL), lambda i, j: (i, 0)),
            pl.BlockSpec((mem_len, D_MODEL), lambda i, j: (i // per_batch, 0)),
            pl.BlockSpec((mem_len, D_MODEL), lambda i, j: (i // per_batch, 1)),
            pl.BlockSpec((D_MODEL, tn), lambda i, j: (0, j)),
            pl.BlockSpec((tm, tn), lambda i, j: (i, j)),
        ],
        out_specs=pl.BlockSpec((tm, tn), lambda i, j: (i, j)),
        out_shape=jax.ShapeDtypeStruct((m, n), F32),
        scratch_shapes=[pltpu.VMEM((tm, D_MODEL), BF16)],
        compiler_params=_params(2),
        name="xattn_out_proj",
    )(q, kv, kv, w, res)


def _mlp_body(x_ref, g_ref, wu_ref, wd_ref, fg_ref, o_ref, xn_ref, *, final_norm):
    f = pl.program_id(1)

    @pl.when(f == 0)
    def _():
        _rmsnorm_rows_to(x_ref, g_ref, xn_ref)

        def copy(i, carry):
            rows = pl.ds(pl.multiple_of(i * ROW_CHUNK, ROW_CHUNK), ROW_CHUNK)
            o_ref[rows, :] = x_ref[rows, :]
            return carry

        lax.fori_loop(0, x_ref.shape[0] // ROW_CHUNK, copy, 0)

    u = jnp.dot(xn_ref[...], wu_ref[...], preferred_element_type=F32)
    a = jnp.square(jnp.maximum(u, 0.0)).astype(BF16)
    o_ref[...] += jnp.dot(a, wd_ref[...], preferred_element_type=F32)

    if final_norm:
        @pl.when(f == pl.num_programs(1) - 1)
        def _():
            def norm(i, carry):
                rows = pl.ds(pl.multiple_of(i * ROW_CHUNK, ROW_CHUNK), ROW_CHUNK)
                y = o_ref[rows, :]
                ms = jnp.mean(y * y, axis=-1, keepdims=True)
                o_ref[rows, :] = y * lax.rsqrt(ms + EPS) * fg_ref[...]
                return carry

            lax.fori_loop(0, o_ref.shape[0] // ROW_CHUNK, norm, 0)


def _mlp(x, g, w_up, w_down, final_g, final_norm, tm, tf):
    m, d = x.shape
    d_ff = w_up.shape[1]
    return pl.pallas_call(
        functools.partial(_mlp_body, final_norm=final_norm),
        grid=(m // tm, d_ff // tf),
        in_specs=[
            pl.BlockSpec((tm, d), lambda i, f: (i, 0)),
            pl.BlockSpec((1, d), lambda i, f: (0, 0)),
            pl.BlockSpec((d, tf), lambda i, f: (0, f)),
            pl.BlockSpec((tf, d), lambda i, f: (f, 0)),
            pl.BlockSpec((1, d), lambda i, f: (0, 0)),
        ],
        out_specs=pl.BlockSpec((tm, d), lambda i, f: (i, 0)),
        out_shape=jax.ShapeDtypeStruct((m, d), F32),
        scratch_shapes=[pltpu.VMEM((tm, d), BF16)],
        compiler_params=_params(2),
        name="mlp",
    )(x, g.reshape(1, d), w_up, w_down, final_g.reshape(1, d))


def _split_in_proj(w_in):
    o = np.cumsum((0, A_W, A_W, A_W, B_QK_W, B_QK_W, B_V_W, 2 * C_W, C_W, C_W, N_GATES))
    wp = jnp.concatenate([w_in[:, o[0]:o[6]], w_in[:, o[7]:o[8]]], axis=1)
    pad = jnp.zeros((w_in.shape[0], HEAD_DIM - N_GATES), w_in.dtype)
    wpf = jnp.concatenate([w_in[:, o[6]:o[7]], w_in[:, o[8]:o[10]], pad], axis=1)
    return wp.astype(BF16), wpf.astype(BF16)


def kernel(x, mem, norm_mix_g, w_in, conv_w, gate_b, diff_lambda, head_norm_g, w_out, norm_x_g, norm_mem_g,
           w_xq, w_xkv, w_xo, norm_mlp_g, w_up, w_down, final_norm_g):
    b, t, d = x.shape
    m = b * t
    depth = w_in.shape[0]
    h = x.reshape(m, d)
    mem2 = mem.reshape(-1, d)
    slopes_a = _alibi_slopes(A_HEADS)
    p_scale = np.ones((P_W,), np.float32)
    p_scale[0:A_W] = A_Q_SCALE
    p_scale[P_BQ:P_BK] = B_Q_SCALE
    p_scale = jnp.asarray(p_scale)
    for layer in range(depth):
        lam_init = 0.8 - 0.6 * math.exp(-0.3 * layer)
        wp, wpf = _split_in_proj(w_in[layer])
        p = _norm_matmul(h, norm_mix_g[layer], wp, BF16, 1024, 768, p_scale).reshape(b, t, P_W)
        pf = _norm_matmul(h, norm_mix_g[layer], wpf, F32, 512, PF_W).reshape(b, t, PF_W)
        cqk = _conv_silu(pf, conv_w[layer])
        oa = _dilated_mixture(p, slopes_a)
        ob = _diff_attention(p, diff_lambda[layer], lam_init)
        oc = _mlstm(cqk, p, pf, gate_b[layer])
        h = _merge_out(oa, ob, oc, head_norm_g[layer], w_out[layer].astype(BF16), h, 1.0 - lam_init, 512, 1024)

        q = _norm_matmul(h, norm_x_g[layer], w_xq[layer].astype(BF16), BF16, 1024, 1024)
        kv = _norm_matmul(mem2, norm_mem_g[layer], w_xkv[layer].astype(BF16), BF16, 512, 1024)
        h = _xattn_out(q, kv, w_xo[layer].astype(BF16), h, t, 512, 1024)

        h = _mlp(h, norm_mlp_g[layer], w_up[layer].astype(BF16), w_down[layer].astype(BF16), final_norm_g,
                 layer == depth - 1, 1024, 512)
    return h.reshape(b, t, d)
```

```python
import functools
import math

import numpy as np
import jax
import jax.numpy as jnp
from jax import lax
from jax.experimental import pallas as pl
from jax.experimental.pallas import tpu as pltpu

F32 = jnp.float32
BF16 = jnp.bfloat16

D_MODEL = 2048
HEAD_DIM = 128
A_HEADS = 6
B_HEADS = 4
C_HEADS = 6
B_QK_DIM = 64
DILATED_CONFIGS = ((128, 1), (512, 4), (2048, 16))
HALF_WINDOW = 64
MLSTM_CHUNK = 128
X_HEADS = 4
X_HEAD_DIM = D_MODEL // X_HEADS
EPS = 1e-6
NEG = -1e30

A_W = A_HEADS * HEAD_DIM
B_QK_W = B_HEADS * 2 * B_QK_DIM
B_V_W = B_HEADS * HEAD_DIM
C_W = C_HEADS * HEAD_DIM
N_GATES = 4 * C_HEADS
N_MIX_HEADS = A_HEADS + B_HEADS + C_HEADS
MIX_W = N_MIX_HEADS * HEAD_DIM

P_W = 3 * A_W + 2 * B_QK_W + B_V_W + C_W
P_BQ = 3 * A_W
P_BK = P_BQ + B_QK_W
P_BV = P_BK + B_QK_W
P_CV = P_BV + B_V_W
PF_CO = 2 * C_W
PF_CG = PF_CO + C_W
PF_W = PF_CG + HEAD_DIM

LOG2E = math.log2(math.e)
A_Q_SCALE = HEAD_DIM ** -0.5 * LOG2E
B_Q_SCALE = B_QK_DIM ** -0.5 * LOG2E

LANE = 128
ROW_CHUNK = 16
NORM_UNROLL = 4
VMEM_LIMIT = 56 * 1024 * 1024


def _params(n_axes, vmem=VMEM_LIMIT):
    return pltpu.CompilerParams(dimension_semantics=("arbitrary",) * n_axes, vmem_limit_bytes=vmem)


def _alibi_slopes(n_heads):
    return jnp.asarray(2.0 ** (-8.0 * np.arange(1, n_heads + 1) / n_heads), dtype=F32)


def _rmsnorm_rows_to(x_ref, g_ref, dst_ref):
    rows = x_ref.shape[0]

    def body(i, carry):
        r0 = pl.multiple_of(i * ROW_CHUNK, ROW_CHUNK)
        x = x_ref[pl.ds(r0, ROW_CHUNK), :]
        ms = jnp.mean(x * x, axis=-1, keepdims=True)
        dst_ref[pl.ds(r0, ROW_CHUNK), :] = (x * lax.rsqrt(ms + EPS) * g_ref[...]).astype(BF16)
        return carry

    lax.fori_loop(0, rows // ROW_CHUNK, body, 0, unroll=NORM_UNROLL)


def _norm_matmul_body(x_ref, g_ref, w_ref, *rest, scaled):
    s_ref = rest[0] if scaled else None
    o_ref, xn_ref = rest[-2:]

    @pl.when(pl.program_id(1) == 0)
    def _():
        _rmsnorm_rows_to(x_ref, g_ref, xn_ref)

    acc = jnp.dot(xn_ref[...], w_ref[...], preferred_element_type=F32)
    if scaled:
        acc = acc * s_ref[...]
    o_ref[...] = acc.astype(o_ref.dtype)


def _norm_matmul(x, g, w, out_dtype, tm, tn, col_scale=None):
    m, k = x.shape
    n = w.shape[1]
    scaled = col_scale is not None
    in_specs = [
        pl.BlockSpec((tm, k), lambda i, j: (i, 0)),
        pl.BlockSpec((1, k), lambda i, j: (0, 0)),
        pl.BlockSpec((k, tn), lambda i, j: (0, j)),
    ]
    args = [x, g.reshape(1, k), w]
    if scaled:
        in_specs.append(pl.BlockSpec((1, tn), lambda i, j: (0, j)))
        args.append(col_scale.reshape(1, n))
    return pl.pallas_call(
        functools.partial(_norm_matmul_body, scaled=scaled),
        grid=(m // tm, n // tn),
        in_specs=in_specs,
        out_specs=pl.BlockSpec((tm, tn), lambda i, j: (i, j)),
        out_shape=jax.ShapeDtypeStruct((m, n), out_dtype),
        scratch_shapes=[pltpu.VMEM((tm, k), BF16)],
        compiler_params=_params(2),
        name="norm_matmul",
    )(*args)


CONV_COLS = 256
CONV_PAD = 8


def _conv_silu_body(x_ref, w_ref, o_ref, pad_ref):
    t = x_ref.shape[0]
    zeros = jnp.zeros((CONV_PAD, CONV_COLS), F32)
    pad_ref[0:CONV_PAD, :] = zeros
    pad_ref[CONV_PAD + t:2 * CONV_PAD + t, :] = zeros
    scale = jnp.where(pl.program_id(1) < C_W // CONV_COLS, HEAD_DIM ** -0.5, 1.0).astype(F32)

    def copy(i, carry):
        r0 = pl.multiple_of(i * LANE, LANE)
        pad_ref[pl.ds(CONV_PAD + r0, LANE), :] = x_ref[pl.ds(r0, LANE), :]
        return carry

    lax.fori_loop(0, t // LANE, copy, 0)
    w = w_ref[...]
    row = lax.broadcasted_iota(jnp.int32, (LANE, CONV_COLS), 0)

    def body(i, carry):
        r0 = pl.multiple_of(i * LANE, LANE)
        cur = pad_ref[pl.ds(CONV_PAD + r0, LANE), :]
        before = pad_ref[pl.ds(r0, CONV_PAD), :]
        after = pad_ref[pl.ds(CONV_PAD + LANE + r0, CONV_PAD), :]
        prev = jnp.where(row == 0, before[CONV_PAD - 1:CONV_PAD, :], pltpu.roll(cur, 1, axis=0))
        nxt = jnp.where(row == LANE - 1, after[0:1, :], pltpu.roll(cur, LANE - 1, axis=0))
        y = prev * w[0:1, :] + cur * w[1:2, :] + nxt * w[2:3, :]
        y = y * jax.nn.sigmoid(y)
        o_ref[pl.ds(r0, LANE), :] = (y * scale).astype(BF16)
        return carry

    lax.fori_loop(0, t // LANE, body, 0)


def _conv_silu(pf, conv_w):
    b, t, _ = pf.shape
    ncol = 2 * C_W // CONV_COLS
    return pl.pallas_call(
        _conv_silu_body,
        grid=(b, ncol),
        in_specs=[
            pl.BlockSpec((None, t, CONV_COLS), lambda i, j: (i, 0, j)),
            pl.BlockSpec((3, CONV_COLS), lambda i, j: (0, j)),
        ],
        out_specs=pl.BlockSpec((None, t, CONV_COLS), lambda i, j: (i, 0, j)),
        out_shape=jax.ShapeDtypeStruct((b, t, 2 * C_W), BF16),
        scratch_shapes=[pltpu.VMEM((t + 2 * CONV_PAD, CONV_COLS), F32)],
        compiler_params=_params(2),
        name="conv_silu",
    )(pf, conv_w)


A_QB = 128
A_KB = A_QB + 2 * HALF_WINDOW


def _dilated_mix_body(slopes_ref, q_ref, k_ref, v_ref, o_ref,
                      qf_ref, kf_ref, vf_ref, m_ref, l_ref, acc_ref, bias_ref):
    t = q_ref.shape[0]
    n_blocks = t // A_QB
    base_slope = slopes_ref[pl.program_id(1)] * LOG2E
    row = lax.broadcasted_iota(jnp.int32, (A_QB, A_KB), 0)
    col = lax.broadcasted_iota(jnp.int32, (A_QB, A_KB), 1)
    n_cfg = len(DILATED_CONFIGS)

    for g, (_, r) in enumerate(DILATED_CONFIGS):
        for variant in range(3):
            delta = jnp.abs(col - row - variant * HALF_WINDOW)
            bias_ref[g * 3 + variant] = jnp.where(delta <= HALF_WINDOW, (-base_slope * float(r)) * delta.astype(F32), NEG)

    def widen(i, carry):
        rows = pl.ds(pl.multiple_of(i * A_QB, A_QB), A_QB)
        qf_ref[rows, :] = q_ref[rows, :].astype(F32)
        kf_ref[rows, :] = k_ref[rows, :].astype(F32)
        vf_ref[rows, :] = v_ref[rows, :].astype(F32)
        return carry

    lax.fori_loop(0, n_blocks, widen, 0)

    for g, (_, r) in enumerate(DILATED_CONFIGS):
        n_sub = t // r
        nqb = n_sub // A_QB

        def rows_of(bi, r=r, n_sub=n_sub, nqb=nqb):
            rho = bi // nqb
            q0 = (bi % nqb) * A_QB
            k0 = jnp.clip(q0 - HALF_WINDOW, 0, n_sub - A_KB)
            if r == 1:
                qrows = pl.ds(pl.multiple_of(q0, A_QB), A_QB)
                krows = pl.ds(pl.multiple_of(k0, HALF_WINDOW), A_KB)
            else:
                qrows = pl.ds(rho + r * q0, A_QB, stride=r)
                krows = pl.ds(rho + r * k0, A_KB, stride=r)
            return qrows, krows, (q0 - k0) // HALF_WINDOW

        def scores(bi, g=g, rows_of=rows_of):
            qrows, krows, variant = rows_of(jnp.minimum(bi, n_blocks - 1))
            q = qf_ref[qrows, :].astype(BF16)
            k = kf_ref[krows, :].astype(BF16)
            s = lax.dot_general(q, k, (((1,), (1,)), ((), ())), preferred_element_type=F32)
            return s + bias_ref[g * 3 + variant]

        def block(bi, s, g=g, rows_of=rows_of, scores=scores):
            s_next = scores(bi + 1)
            qrows, krows, _ = rows_of(bi)
            v = vf_ref[krows, :].astype(BF16)
            m = jnp.max(s, axis=-1, keepdims=True)
            p = jnp.exp2(s - m)
            den = jnp.sum(p, axis=-1, keepdims=True)
            o = jnp.dot(p.astype(BF16), v, preferred_element_type=F32) / den
            lse = jnp.broadcast_to(m + jnp.log2(den), (A_QB, HEAD_DIM))
            if g == 0:
                m_ref[qrows, :] = lse
                l_ref[qrows, :] = jnp.ones((A_QB, HEAD_DIM), F32)
                acc_ref[qrows, :] = o
            else:
                m_old = m_ref[qrows, :]
                m_new = jnp.maximum(m_old, lse)
                a = jnp.exp2(m_old - m_new)
                w = jnp.exp2(lse - m_new)
                l_new = a * l_ref[qrows, :] + w
                acc_new = a * acc_ref[qrows, :] + w * o
                if g == n_cfg - 1:
                    o_ref[qrows, :] = acc_new / l_new
                else:
                    m_ref[qrows, :] = m_new
                    l_ref[qrows, :] = l_new
                    acc_ref[qrows, :] = acc_new
            return s_next

        lax.fori_loop(0, n_blocks, block, scores(0), unroll=4)


def _dilated_mixture(p, slopes):
    b, t, _ = p.shape
    spec = lambda off: pl.BlockSpec((None, t, HEAD_DIM), lambda i, h: (i, 0, off + h))
    seq = lambda: pltpu.VMEM((t, HEAD_DIM), F32)
    return pl.pallas_call(
        _dilated_mix_body,
        grid=(b, A_HEADS),
        in_specs=[pl.BlockSpec(memory_space=pltpu.SMEM), spec(0), spec(A_HEADS), spec(2 * A_HEADS)],
        out_specs=pl.BlockSpec((None, t, HEAD_DIM), lambda i, h: (i, 0, h)),
        out_shape=jax.ShapeDtypeStruct((b, t, A_W), F32),
        scratch_shapes=[seq(), seq(), seq(), seq(), seq(), seq(),
                        pltpu.VMEM((3 * len(DILATED_CONFIGS), A_QB, A_KB), F32)],
        compiler_params=_params(2),
        name="dilated_mixture",
    )(slopes, p, p, p)


B_BLK = 256
B_AHEAD = 2


def _diff_bias_coefs():
    c = (2.0 ** (-8.0 * np.arange(1, B_HEADS + 1) / B_HEADS)).astype(np.float64) * LOG2E
    out = np.zeros((B_HEADS, 4), np.float32)
    rest = c.copy()
    for i in range(3):
        term = rest.astype(np.float32).astype(BF16).astype(np.float64)
        out[:, i] = term
        rest = rest - term
    out[:, 3] = c
    return out


def _diff_body(coef_ref, lam_ref, q_ref, k_ref, v_ref, o_ref, kaug_ref, vt_ref, *, lam_init):
    h = pl.program_id(1)
    qi = pl.program_id(2)
    nkb = k_ref.shape[0] // B_BLK
    c1, c2, c3, c = coef_ref[h, 0], coef_ref[h, 1], coef_ref[h, 2], coef_ref[h, 3]
    lane = lax.broadcasted_iota(jnp.int32, (B_BLK, LANE), 1)
    pos = lax.broadcasted_iota(jnp.int32, (B_BLK, LANE), 0).astype(F32)
    coefs = jnp.where(lane % 3 == 0, c1, jnp.where(lane % 3 == 1, c2, c3))

    @pl.when(qi == 0)
    def _():
        feat = jnp.where(lane < 3, -pos, jnp.where(lane < 6, coefs, 0.0)).astype(BF16)

        def fill(i, carry):
            rows = pl.ds(pl.multiple_of(i * B_BLK, B_BLK), B_BLK)
            kaug_ref[i, :, 0:LANE] = k_ref[rows, :]
            kaug_ref[i, :, LANE:2 * LANE] = feat
            vt_ref[i] = v_ref[rows, :].astype(F32).T.astype(BF16)
            return carry

        lax.fori_loop(0, nkb, fill, 0)

    qfeat = jnp.where(lane < 3, coefs, jnp.where(lane < 6, pos, 0.0))
    qfeat_right = qfeat.astype(BF16)
    qfeat_left = (-qfeat).astype(BF16)
    q = q_ref[...]
    zero = jnp.zeros_like(q)
    q1 = jnp.where(lane < B_QK_DIM, q, zero)
    q2 = jnp.where(lane >= B_QK_DIM, q, zero)
    nt = (((1,), (1,)), ((), ()))

    def scores(kb, qf):
        q_aug = jnp.concatenate([jnp.concatenate([q1, qf], axis=1), jnp.concatenate([q2, qf], axis=1)], axis=0)
        return lax.dot_general(kaug_ref[kb], q_aug, nt, preferred_element_type=F32)

    s = jnp.minimum(scores(qi, qfeat_right), scores(qi, qfeat_left))
    m_run = jnp.max(s, axis=0, keepdims=True)
    p = jnp.exp2(s - m_run)
    den = jnp.sum(p, axis=0, keepdims=True)
    acc = jnp.dot(vt_ref[qi], p.astype(BF16), preferred_element_type=F32)

    def off_diagonal(n):
        kb = n + (n >= qi).astype(jnp.int32)
        return kb, scores(kb, jnp.where(n < qi, qfeat_left, qfeat_right))

    pending = [off_diagonal(n) for n in range(B_AHEAD)]
    for n in range(nkb - 1):
        kb, s = pending.pop(0)
        if n + B_AHEAD < nkb - 1:
            pending.append(off_diagonal(n + B_AHEAD))
        shift = (-c * B_BLK) * jnp.abs(kb - qi).astype(F32)
        m_new = jnp.maximum(m_run, jnp.max(s, axis=0, keepdims=True) + shift)
        alpha = jnp.exp2(m_run - m_new)
        p = jnp.exp2(s - (m_new - shift))
        den = alpha * den + jnp.sum(p, axis=0, keepdims=True)
        acc = alpha * acc + jnp.dot(vt_ref[kb], p.astype(BF16), preferred_element_type=F32)
        m_run = m_new

    dl = lam_ref[...]
    lam = (jnp.exp(jnp.sum(dl[0:1, :] * dl[1:2, :], axis=-1, keepdims=True))
           - jnp.exp(jnp.sum(dl[2:3, :] * dl[3:4, :], axis=-1, keepdims=True)) + lam_init)
    o = acc / den
    o_ref[...] = (o[:, 0:B_BLK] - lam * o[:, B_BLK:2 * B_BLK]).T


def _diff_attention(p, diff_lambda, lam_init):
    b, t, _ = p.shape
    nkb = t // B_BLK
    qoff, koff, voff = P_BQ // HEAD_DIM, P_BK // HEAD_DIM, P_BV // HEAD_DIM
    return pl.pallas_call(
        functools.partial(_diff_body, lam_init=lam_init),
        grid=(b, B_HEADS, nkb),
        in_specs=[
            pl.BlockSpec(memory_space=pltpu.SMEM),
            pl.BlockSpec((4, B_QK_DIM), lambda i, h, qi: (0, 0)),
            pl.BlockSpec((None, B_BLK, HEAD_DIM), lambda i, h, qi: (i, qi, qoff + h)),
            pl.BlockSpec((None, t, HEAD_DIM), lambda i, h, qi: (i, 0, koff + h)),
            pl.BlockSpec((None, t, HEAD_DIM), lambda i, h, qi: (i, 0, voff + h)),
        ],
        out_specs=pl.BlockSpec((None, B_BLK, HEAD_DIM), lambda i, h, qi: (i, qi, h)),
        out_shape=jax.ShapeDtypeStruct((b, t, B_V_W), F32),
        scratch_shapes=[pltpu.VMEM((nkb, B_BLK, 2 * LANE), BF16), pltpu.VMEM((nkb, HEAD_DIM, B_BLK), BF16)],
        compiler_params=_params(3),
        name="diff_attention",
    )(jnp.asarray(_diff_bias_coefs()), diff_lambda, p, p, p)


def _log_sigmoid(x):
    return jnp.minimum(x, 0.0) - jnp.log(1.0 + jnp.exp(-jnp.abs(x)))


C_HEADS_PER_STEP = 3


def _mlstm_body(q_ref, k_ref, v_ref, og_ref, gcol_ref, grow_ref, bcol_ref, brow_ref, o_ref, c_ref):
    lc = MLSTM_CHUNK
    t = q_ref.shape[0]
    nc = t // lc
    heads = range(C_HEADS_PER_STEP)
    row = lax.broadcasted_iota(jnp.int32, (lc, lc), 0)
    col = lax.broadcasted_iota(jnp.int32, (lc, lc), 1)
    lower = col <= row
    upper = col >= row
    tri_lower = lower.astype(F32)
    tri_upper = upper.astype(F32)
    ones_col = (lax.broadcasted_iota(jnp.int32, (lc, LANE), 1) == 0).astype(BF16)
    c_ref[...] = jnp.zeros(c_ref.shape, F32)

    nt = (((1,), (1,)), ((), ()))

    def make_step(first_touch):
        def step(i, carry):
            scans = [(hh, d, ci) for hh in heads for d, ci in ((0, i), (1, nc - 1 - i))]
            ns = len(scans)
            qs, ks, vs, s_qk, s_qc, gates = [], [], [], [], [], []
            for hh, d, ci in scans:
                rows = pl.ds(pl.multiple_of(ci * lc, lc), lc)
                cols = slice(hh * HEAD_DIM, (hh + 1) * HEAD_DIM)
                q, k, v = q_ref[rows, cols], k_ref[rows, cols], v_ref[rows, cols]
                qs.append(q), ks.append(k), vs.append(v)
                s_qk.append(lax.dot_general(q, k, nt, preferred_element_type=F32))
                s_qc.append(jnp.dot(q, c_ref[2 * hh + d].astype(BF16), preferred_element_type=F32))

            for hh, d, ci in scans:
                cum_col, cum_row = (tri_lower, tri_upper) if d == 0 else (tri_upper, tri_lower)
                rows = pl.ds(pl.multiple_of(ci * lc, lc), lc)
                gc = gcol_ref[rows, :] + bcol_ref[...]
                gr = grow_ref[hh, ci] + brow_ref[hh]
                li_c = gc[:, 4 * hh + 2 * d:4 * hh + 2 * d + 1]
                lf_c = _log_sigmoid(gc[:, 4 * hh + 2 * d + 1:4 * hh + 2 * d + 2])
                li_r = gr[2 * d:2 * d + 1, :]
                lf_r = _log_sigmoid(gr[2 * d + 1:2 * d + 2, :])
                b_c = jnp.sum(cum_col * lf_r, axis=1, keepdims=True)
                b_r = jnp.sum(cum_row * lf_c, axis=0, keepdims=True)
                b_last = jnp.sum(lf_r, axis=-1, keepdims=True)
                gates.append((li_c, li_r, b_c, b_r, b_last))

            scs, m_ts, iws = [], [], []
            for n, (hh, d, ci) in enumerate(scans):
                li_c, li_r, b_c, b_r, b_last = gates[n]
                dmat = jnp.where(lower if d == 0 else upper, b_c - b_r + li_r, NEG)
                inter = b_c + carry[n]
                m_t = jnp.maximum(inter, jnp.max(dmat, axis=-1, keepdims=True))
                scs.append(s_qk[n] * jnp.exp(dmat - m_t))
                m_ts.append(m_t), iws.append(jnp.exp(inter - m_t))

            s_sv = [jnp.dot(scs[n].astype(BF16), vs[n], preferred_element_type=F32) for n in range(ns)]

            new_carry, kg_ts, decays = [], [], []
            for n, (hh, d, ci) in enumerate(scans):
                li_c, li_r, b_c, b_r, b_last = gates[n]
                num = iws[n] * s_qc[n][:, 0:HEAD_DIM] + s_sv[n]
                den = iws[n] * s_qc[n][:, HEAD_DIM:HEAD_DIM + 1] + jnp.sum(scs[n], axis=-1, keepdims=True)
                h_new = num / jnp.maximum(jnp.abs(den), jnp.exp(-m_ts[n]))
                rows = pl.ds(pl.multiple_of(ci * lc, lc), lc)
                cols = slice(hh * HEAD_DIM, (hh + 1) * HEAD_DIM)
                if first_touch:
                    o_ref[rows, cols] = h_new
                else:
                    o_ref[rows, cols] = (o_ref[rows, cols] + h_new) * jax.nn.sigmoid(og_ref[rows, cols])

                g_c = b_last - b_c + li_c
                g_r = b_last - b_r + li_r
                m_new = jnp.maximum(b_last + carry[n], jnp.max(g_r, axis=-1, keepdims=True))
                decays.append(jnp.exp(b_last + carry[n] - m_new))
                kg_ts.append((ks[n].astype(F32) * jnp.exp(g_c - m_new)).T.astype(BF16))
                new_carry.append(m_new)

            for n, (hh, d, ci) in enumerate(scans):
                v_ext = jnp.concatenate([vs[n], ones_col], axis=1)
                c_ref[2 * hh + d] = decays[n] * c_ref[2 * hh + d] + jnp.dot(kg_ts[n], v_ext, preferred_element_type=F32)
            return tuple(new_carry)
        return step

    zero = jnp.zeros((1, 1), F32)
    carry = lax.fori_loop(0, nc // 2, make_step(True), (zero,) * (2 * C_HEADS_PER_STEP))
    lax.fori_loop(nc // 2, nc, make_step(False), carry)


def _mlstm(cqk, p, pf, gate_b):
    b, t, _ = p.shape
    hb = C_HEADS_PER_STEP
    width = hb * HEAD_DIM
    groups = C_HEADS // hb
    nc = t // MLSTM_CHUNK
    cg = pf[:, :, PF_CG:PF_CG + N_GATES].reshape(b, t, 4, C_HEADS)
    gcol = cg.reshape(b, t, 4, groups, hb).transpose(0, 3, 1, 4, 2).reshape(b, groups, t, 4 * hb)
    grow = cg.reshape(b, nc, MLSTM_CHUNK, 4, C_HEADS).transpose(0, 4, 1, 3, 2)
    gb = gate_b.reshape(4, C_HEADS).T
    bcol = gb.reshape(groups, 1, 4 * hb)
    brow = gb.reshape(C_HEADS, 4, 1)
    voff, ooff = P_CV // width, PF_CO // width
    return pl.pallas_call(
        _mlstm_body,
        grid=(b, groups),
        in_specs=[
            pl.BlockSpec((None, t, width), lambda i, h: (i, 0, h)),
            pl.BlockSpec((None, t, width), lambda i, h: (i, 0, groups + h)),
            pl.BlockSpec((None, t, width), lambda i, h: (i, 0, voff + h)),
            pl.BlockSpec((None, t, width), lambda i, h: (i, 0, ooff + h)),
            pl.BlockSpec((None, None, t, 4 * hb), lambda i, h: (i, h, 0, 0)),
            pl.BlockSpec((None, hb, nc, 4, MLSTM_CHUNK), lambda i, h: (i, h, 0, 0, 0)),
            pl.BlockSpec((None, 1, 4 * hb), lambda i, h: (h, 0, 0)),
            pl.BlockSpec((hb, 4, 1), lambda i, h: (h, 0, 0)),
        ],
        out_specs=pl.BlockSpec((None, t, width), lambda i, h: (i, 0, h)),
        out_shape=jax.ShapeDtypeStruct((b, t, C_W), F32),
        scratch_shapes=[pltpu.VMEM((2 * hb, HEAD_DIM, HEAD_DIM + LANE), F32)],
        compiler_params=_params(2),
        name="mlstm",
    )(cqk, cqk, p, pf, gcol, grow, bcol, brow)


MERGE_ROWS = 32


def _merge_out_body(oa_ref, ob_ref, oc_ref, g_ref, w_ref, res_ref, o_ref, lhs_ref, *, b_scale):
    @pl.when(pl.program_id(1) == 0)
    def _():
        def body(i, carry):
            rows = pl.ds(pl.multiple_of(i * MERGE_ROWS, MERGE_ROWS), MERGE_ROWS)
            head = 0
            for src_ref, n_heads, scale in ((oa_ref, A_HEADS, None), (ob_ref, B_HEADS, b_scale), (oc_ref, C_HEADS, None)):
                for h in range(n_heads):
                    o = src_ref[rows, h * HEAD_DIM:(h + 1) * HEAD_DIM]
                    ms = jnp.mean(o * o, axis=-1, keepdims=True)
                    y = o * lax.rsqrt(ms + EPS) * g_ref[:, head * HEAD_DIM:(head + 1) * HEAD_DIM]
                    if scale is not None:
                        y = y * scale
                    lhs_ref[rows, head * HEAD_DIM:(head + 1) * HEAD_DIM] = y.astype(BF16)
                    head += 1
            return carry

        lax.fori_loop(0, lhs_ref.shape[0] // MERGE_ROWS, body, 0, unroll=2)

    o_ref[...] = res_ref[...] + jnp.dot(lhs_ref[...], w_ref[...], preferred_element_type=F32)


def _merge_out(oa, ob, oc, g, w, res, b_scale, tm, tn):
    m, n = res.shape
    row = lambda width: pl.BlockSpec((tm, width), lambda i, j: (i, 0))
    return pl.pallas_call(
        functools.partial(_merge_out_body, b_scale=b_scale),
        grid=(m // tm, n // tn),
        in_specs=[
            row(A_W), row(B_V_W), row(C_W),
            pl.BlockSpec((1, MIX_W), lambda i, j: (0, 0)),
            pl.BlockSpec((MIX_W, tn), lambda i, j: (0, j)),
            pl.BlockSpec((tm, tn), lambda i, j: (i, j)),
        ],
        out_specs=pl.BlockSpec((tm, tn), lambda i, j: (i, j)),
        out_shape=jax.ShapeDtypeStruct((m, n), F32),
        scratch_shapes=[pltpu.VMEM((tm, MIX_W), BF16)],
        compiler_params=_params(2),
        name="merge_out_proj",
    )(oa.reshape(m, A_W), ob.reshape(m, B_V_W), oc.reshape(m, C_W), g.reshape(1, MIX_W), w, res)


def _xattn_out_body(q_ref, k_ref, v_ref, w_ref, res_ref, o_ref, lhs_ref):
    @pl.when(pl.program_id(1) == 0)
    def _():
        for h in range(X_HEADS):
            cols = slice(h * X_HEAD_DIM, (h + 1) * X_HEAD_DIM)
            s = lax.dot_general(q_ref[:, cols], k_ref[:, cols], (((1,), (1,)), ((), ())),
                                preferred_element_type=F32) * (X_HEAD_DIM ** -0.5)
            m = jnp.max(s, axis=-1, keepdims=True)
            p = jnp.exp(s - m)
            a = p / jnp.sum(p, axis=-1, keepdims=True)
            lhs_ref[:, cols] = jnp.dot(a.astype(BF16), v_ref[:, cols], preferred_element_type=F32).astype(BF16)

    o_ref[...] = res_ref[...] + jnp.dot(lhs_ref[...], w_ref[...], preferred_element_type=F32)


def _xattn_out(q, kv, w, res, seq, tm, tn):
    m, n = res.shape
    mem_len = kv.shape[0] // (m // seq)
    per_batch = seq // tm
    return pl.pallas_call(
        _xattn_out_body,
        grid=(m // tm, n // tn),
        in_specs=[
            pl.BlockSpec((tm, D_MODEL), lambda i, j: (i, 0)),
            pl.BlockSpec((mem_len, D_MODEL), lambda i, j: (i // per_batch, 0)),
            pl.BlockSpec((mem_len, D_MODEL), lambda i, j: (i // per_batch, 1)),
            pl.BlockSpec((D_MODEL, tn), lambda i, j: (0, j)),
            pl.BlockSpec((tm, tn), lambda i, j: (i, j)),
        ],
        out_specs=pl.BlockSpec((tm, tn), lambda i, j: (i, j)),
        out_shape=jax.ShapeDtypeStruct((m, n), F32),
        scratch_shapes=[pltpu.VMEM((tm, D_MODEL), BF16)],
        compiler_params=_params(2),
        name="xattn_out_proj",
    )(q, kv, kv, w, res)


def _mlp_body(x_ref, g_ref, wu_ref, wd_ref, fg_ref, o_ref, xn_ref, *, final_norm):
    f = pl.program_id(1)

    @pl.when(f == 0)
    def _():
        _rmsnorm_rows_to(x_ref, g_ref, xn_ref)

        def copy(i, carry):
            rows = pl.ds(pl.multiple_of(i * ROW_CHUNK, ROW_CHUNK), ROW_CHUNK)
            o_ref[rows, :] = x_ref[rows, :]
            return carry

        lax.fori_loop(0, x_ref.shape[0] // ROW_CHUNK, copy, 0, unroll=NORM_UNROLL)

    u = jnp.dot(xn_ref[...], wu_ref[...], preferred_element_type=F32)
    a = jnp.square(jnp.maximum(u, 0.0)).astype(BF16)
    o_ref[...] += jnp.dot(a, wd_ref[...], preferred_element_type=F32)

    if final_norm:
        @pl.when(f == pl.num_programs(1) - 1)
        def _():
            def norm(i, carry):
                rows = pl.ds(pl.multiple_of(i * ROW_CHUNK, ROW_CHUNK), ROW_CHUNK)
                y = o_ref[rows, :]
                ms = jnp.mean(y * y, axis=-1, keepdims=True)
                o_ref[rows, :] = y * lax.rsqrt(ms + EPS) * fg_ref[...]
                return carry

            lax.fori_loop(0, o_ref.shape[0] // ROW_CHUNK, norm, 0, unroll=NORM_UNROLL)


def _mlp(x, g, w_up, w_down, final_g, final_norm, tm, tf):
    m, d = x.shape
    d_ff = w_up.shape[1]
    return pl.pallas_call(
        functools.partial(_mlp_body, final_norm=final_norm),
        grid=(m // tm, d_ff // tf),
        in_specs=[
            pl.BlockSpec((tm, d), lambda i, f: (i, 0)),
            pl.BlockSpec((1, d), lambda i, f: (0, 0)),
            pl.BlockSpec((d, tf), lambda i, f: (0, f)),
            pl.BlockSpec((tf, d), lambda i, f: (f, 0)),
            pl.BlockSpec((1, d), lambda i, f: (0, 0)),
        ],
        out_specs=pl.BlockSpec((tm, d), lambda i, f: (i, 0)),
        out_shape=jax.ShapeDtypeStruct((m, d), F32),
        scratch_shapes=[pltpu.VMEM((tm, d), BF16)],
        compiler_params=_params(2),
        name="mlp",
    )(x, g.reshape(1, d), w_up, w_down, final_g.reshape(1, d))


def _split_in_proj(w_in):
    o = np.cumsum((0, A_W, A_W, A_W, B_QK_W, B_QK_W, B_V_W, 2 * C_W, C_W, C_W, N_GATES))
    wp = jnp.concatenate([w_in[:, o[0]:o[6]], w_in[:, o[7]:o[8]]], axis=1)
    pad = jnp.zeros((w_in.shape[0], HEAD_DIM - N_GATES), w_in.dtype)
    wpf = jnp.concatenate([w_in[:, o[6]:o[7]], w_in[:, o[8]:o[10]], pad], axis=1)
    return wp.astype(BF16), wpf.astype(BF16)


def kernel(x, mem, norm_mix_g, w_in, conv_w, gate_b, diff_lambda, head_norm_g, w_out, norm_x_g, norm_mem_g,
           w_xq, w_xkv, w_xo, norm_mlp_g, w_up, w_down, final_norm_g):
    b, t, d = x.shape
    m = b * t
    depth = w_in.shape[0]
    h = x.reshape(m, d)
    mem2 = mem.reshape(-1, d)
    slopes_a = _alibi_slopes(A_HEADS)
    p_scale = np.ones((P_W,), np.float32)
    p_scale[0:A_W] = A_Q_SCALE
    p_scale[P_BQ:P_BK] = B_Q_SCALE
    p_scale = jnp.asarray(p_scale)
    for layer in range(depth):
        lam_init = 0.8 - 0.6 * math.exp(-0.3 * layer)
        wp, wpf = _split_in_proj(w_in[layer])
        p = _norm_matmul(h, norm_mix_g[layer], wp, BF16, 1024, 768, p_scale).reshape(b, t, P_W)
        pf = _norm_matmul(h, norm_mix_g[layer], wpf, F32, 512, PF_W).reshape(b, t, PF_W)
        cqk = _conv_silu(pf, conv_w[layer])
        oa = _dilated_mixture(p, slopes_a)
        ob = _diff_attention(p, diff_lambda[layer], lam_init)
        oc = _mlstm(cqk, p, pf, gate_b[layer])
        h = _merge_out(oa, ob, oc, head_norm_g[layer], w_out[layer].astype(BF16), h, 1.0 - lam_init, 512, 1024)

        q = _norm_matmul(h, norm_x_g[layer], w_xq[layer].astype(BF16), BF16, 1024, 1024)
        kv = _norm_matmul(mem2, norm_mem_g[layer], w_xkv[layer].astype(BF16), BF16, 512, 1024)
        h = _xattn_out(q, kv, w_xo[layer].astype(BF16), h, t, 512, 1024)

        h = _mlp(h, norm_mlp_g[layer], w_up[layer].astype(BF16), w_down[layer].astype(BF16), final_norm_g,
                 layer == depth - 1, 1024, 512)
    return h.reshape(b, t, d)
```

```python
import functools
import math

import numpy as np
import jax
import jax.numpy as jnp
from jax import lax
from jax.experimental import pallas as pl
from jax.experimental.pallas import tpu as pltpu

F32 = jnp.float32
BF16 = jnp.bfloat16

D_MODEL = 2048
HEAD_DIM = 128
A_HEADS = 6
B_HEADS = 4
C_HEADS = 6
B_QK_DIM = 64
DILATED_CONFIGS = ((128, 1), (512, 4), (2048, 16))
HALF_WINDOW = 64
MLSTM_CHUNK = 128
X_HEADS = 4
X_HEAD_DIM = D_MODEL // X_HEADS
EPS = 1e-6
NEG = -1e30

A_W = A_HEADS * HEAD_DIM
B_QK_W = B_HEADS * 2 * B_QK_DIM
B_V_W = B_HEADS * HEAD_DIM
C_W = C_HEADS * HEAD_DIM
N_GATES = 4 * C_HEADS
N_MIX_HEADS = A_HEADS + B_HEADS + C_HEADS
MIX_W = N_MIX_HEADS * HEAD_DIM

P_W = 3 * A_W + 2 * B_QK_W + B_V_W + C_W
P_BQ = 3 * A_W
P_BK = P_BQ + B_QK_W
P_BV = P_BK + B_QK_W
P_CV = P_BV + B_V_W
PF_CO = 2 * C_W
PF_CG = PF_CO + C_W
PF_W = PF_CG + HEAD_DIM

LOG2E = math.log2(math.e)
A_Q_SCALE = HEAD_DIM ** -0.5 * LOG2E
B_Q_SCALE = B_QK_DIM ** -0.5 * LOG2E

LANE = 128
ROW_CHUNK = 16
NORM_UNROLL = 4
CAST_BLOCK_BYTES = 4 * 1024 * 1024
VMEM_LIMIT = 56 * 1024 * 1024


def _params(n_axes, vmem=VMEM_LIMIT):
    return pltpu.CompilerParams(dimension_semantics=("arbitrary",) * n_axes, vmem_limit_bytes=vmem)


def _alibi_slopes(n_heads):
    return jnp.asarray(2.0 ** (-8.0 * np.arange(1, n_heads + 1) / n_heads), dtype=F32)


def _weight_spec(w, block, index):
    if isinstance(w, tuple):
        layer, _ = w
        return pl.BlockSpec((None,) + block, lambda *g: (layer,) + index(*g))
    return pl.BlockSpec(block, index)


def _weight_arg(w):
    return w[1] if isinstance(w, tuple) else w


def _weight_shape(w):
    return w[1].shape[1:] if isinstance(w, tuple) else w.shape


def _cast_body(x_ref, o_ref):
    o_ref[...] = x_ref[...].astype(o_ref.dtype)


def _to_bf16(w):
    depth, k, n = w.shape
    tk = k
    while tk * n * 4 > CAST_BLOCK_BYTES and tk % 16 == 0:
        tk //= 2
    spec = pl.BlockSpec((None, tk, n), lambda l, i: (l, i, 0))
    return pl.pallas_call(
        _cast_body,
        grid=(depth, k // tk),
        in_specs=[spec],
        out_specs=spec,
        out_shape=jax.ShapeDtypeStruct(w.shape, BF16),
        compiler_params=_params(2),
        name="cast_bf16",
    )(w)


def _rmsnorm_rows_to(x_ref, g_ref, dst_ref):
    rows = x_ref.shape[0]

    def body(i, carry):
        r0 = pl.multiple_of(i * ROW_CHUNK, ROW_CHUNK)
        x = x_ref[pl.ds(r0, ROW_CHUNK), :]
        ms = jnp.mean(x * x, axis=-1, keepdims=True)
        dst_ref[pl.ds(r0, ROW_CHUNK), :] = (x * lax.rsqrt(ms + EPS) * g_ref[...]).astype(BF16)
        return carry

    lax.fori_loop(0, rows // ROW_CHUNK, body, 0, unroll=NORM_UNROLL)


def _norm_matmul_body(x_ref, g_ref, w_ref, *rest, scaled):
    s_ref = rest[0] if scaled else None
    o_ref, xn_ref = rest[-2:]

    @pl.when(pl.program_id(1) == 0)
    def _():
        _rmsnorm_rows_to(x_ref, g_ref, xn_ref)

    acc = jnp.dot(xn_ref[...], w_ref[...], preferred_element_type=F32)
    if scaled:
        acc = acc * s_ref[...]
    o_ref[...] = acc.astype(o_ref.dtype)


def _norm_matmul(x, g, w, out_dtype, tm, tn, col_scale=None):
    m, k = x.shape
    n = _weight_shape(w)[1]
    scaled = col_scale is not None
    in_specs = [
        pl.BlockSpec((tm, k), lambda i, j: (i, 0)),
        pl.BlockSpec((1, k), lambda i, j: (0, 0)),
        _weight_spec(w, (k, tn), lambda i, j: (0, j)),
    ]
    args = [x, g.reshape(1, k), _weight_arg(w)]
    if scaled:
        in_specs.append(pl.BlockSpec((1, tn), lambda i, j: (0, j)))
        args.append(col_scale.reshape(1, n))
    return pl.pallas_call(
        functools.partial(_norm_matmul_body, scaled=scaled),
        grid=(m // tm, n // tn),
        in_specs=in_specs,
        out_specs=pl.BlockSpec((tm, tn), lambda i, j: (i, j)),
        out_shape=jax.ShapeDtypeStruct((m, n), out_dtype),
        scratch_shapes=[pltpu.VMEM((tm, k), BF16)],
        compiler_params=_params(2),
        name="norm_matmul",
    )(*args)


CONV_COLS = 256
CONV_PAD = 8


def _conv_silu_body(x_ref, w_ref, o_ref, pad_ref):
    t = x_ref.shape[0]
    zeros = jnp.zeros((CONV_PAD, CONV_COLS), F32)
    pad_ref[0:CONV_PAD, :] = zeros
    pad_ref[CONV_PAD + t:2 * CONV_PAD + t, :] = zeros
    scale = jnp.where(pl.program_id(1) < C_W // CONV_COLS, HEAD_DIM ** -0.5, 1.0).astype(F32)

    def copy(i, carry):
        r0 = pl.multiple_of(i * LANE, LANE)
        pad_ref[pl.ds(CONV_PAD + r0, LANE), :] = x_ref[pl.ds(r0, LANE), :]
        return carry

    lax.fori_loop(0, t // LANE, copy, 0)
    w = w_ref[...]
    row = lax.broadcasted_iota(jnp.int32, (LANE, CONV_COLS), 0)

    def body(i, carry):
        r0 = pl.multiple_of(i * LANE, LANE)
        cur = pad_ref[pl.ds(CONV_PAD + r0, LANE), :]
        before = pad_ref[pl.ds(r0, CONV_PAD), :]
        after = pad_ref[pl.ds(CONV_PAD + LANE + r0, CONV_PAD), :]
        prev = jnp.where(row == 0, before[CONV_PAD - 1:CONV_PAD, :], pltpu.roll(cur, 1, axis=0))
        nxt = jnp.where(row == LANE - 1, after[0:1, :], pltpu.roll(cur, LANE - 1, axis=0))
        y = prev * w[0:1, :] + cur * w[1:2, :] + nxt * w[2:3, :]
        y = y * jax.nn.sigmoid(y)
        o_ref[pl.ds(r0, LANE), :] = (y * scale).astype(BF16)
        return carry

    lax.fori_loop(0, t // LANE, body, 0)


def _conv_silu(pf, conv_w):
    b, t, _ = pf.shape
    ncol = 2 * C_W // CONV_COLS
    return pl.pallas_call(
        _conv_silu_body,
        grid=(b, ncol),
        in_specs=[
            pl.BlockSpec((None, t, CONV_COLS), lambda i, j: (i, 0, j)),
            pl.BlockSpec((3, CONV_COLS), lambda i, j: (0, j)),
        ],
        out_specs=pl.BlockSpec((None, t, CONV_COLS), lambda i, j: (i, 0, j)),
        out_shape=jax.ShapeDtypeStruct((b, t, 2 * C_W), BF16),
        scratch_shapes=[pltpu.VMEM((t + 2 * CONV_PAD, CONV_COLS), F32)],
        compiler_params=_params(2),
        name="conv_silu",
    )(pf, conv_w)


A_QB = 128
A_KB = A_QB + 2 * HALF_WINDOW


def _dilated_mix_body(slopes_ref, q_ref, k_ref, v_ref, o_ref,
                      qf_ref, kf_ref, vf_ref, m_ref, l_ref, acc_ref, bias_ref):
    t = q_ref.shape[0]
    n_blocks = t // A_QB
    base_slope = slopes_ref[pl.program_id(1)] * LOG2E
    row = lax.broadcasted_iota(jnp.int32, (A_QB, A_KB), 0)
    col = lax.broadcasted_iota(jnp.int32, (A_QB, A_KB), 1)
    n_cfg = len(DILATED_CONFIGS)

    for g, (_, r) in enumerate(DILATED_CONFIGS):
        for variant in range(3):
            delta = jnp.abs(col - row - variant * HALF_WINDOW)
            bias_ref[g * 3 + variant] = jnp.where(delta <= HALF_WINDOW, (-base_slope * float(r)) * delta.astype(F32), NEG)

    def widen(i, carry):
        rows = pl.ds(pl.multiple_of(i * A_QB, A_QB), A_QB)
        qf_ref[rows, :] = q_ref[rows, :].astype(F32)
        kf_ref[rows, :] = k_ref[rows, :].astype(F32)
        vf_ref[rows, :] = v_ref[rows, :].astype(F32)
        return carry

    lax.fori_loop(0, n_blocks, widen, 0)

    for g, (_, r) in enumerate(DILATED_CONFIGS):
        n_sub = t // r
        nqb = n_sub // A_QB

        def rows_of(bi, r=r, n_sub=n_sub, nqb=nqb):
            rho = bi // nqb
            q0 = (bi % nqb) * A_QB
            k0 = jnp.clip(q0 - HALF_WINDOW, 0, n_sub - A_KB)
            if r == 1:
                qrows = pl.ds(pl.multiple_of(q0, A_QB), A_QB)
                krows = pl.ds(pl.multiple_of(k0, HALF_WINDOW), A_KB)
            else:
                qrows = pl.ds(rho + r * q0, A_QB, stride=r)
                krows = pl.ds(rho + r * k0, A_KB, stride=r)
            return qrows, krows, (q0 - k0) // HALF_WINDOW

        def scores(bi, g=g, rows_of=rows_of):
            qrows, krows, variant = rows_of(jnp.minimum(bi, n_blocks - 1))
            q = qf_ref[qrows, :].astype(BF16)
            k = kf_ref[krows, :].astype(BF16)
            s = lax.dot_general(q, k, (((1,), (1,)), ((), ())), preferred_element_type=F32)
            return s + bias_ref[g * 3 + variant]

        def block(bi, s, g=g, rows_of=rows_of, scores=scores):
            s_next = scores(bi + 1)
            qrows, krows, _ = rows_of(bi)
            v = vf_ref[krows, :].astype(BF16)
            m = jnp.max(s, axis=-1, keepdims=True)
            p = jnp.exp2(s - m)
            den = jnp.sum(p, axis=-1, keepdims=True)
            o = jnp.dot(p.astype(BF16), v, preferred_element_type=F32) / den
            lse = jnp.broadcast_to(m + jnp.log2(den), (A_QB, HEAD_DIM))
            if g == 0:
                m_ref[qrows, :] = lse
                l_ref[qrows, :] = jnp.ones((A_QB, HEAD_DIM), F32)
                acc_ref[qrows, :] = o
            else:
                m_old = m_ref[qrows, :]
                m_new = jnp.maximum(m_old, lse)
                a = jnp.exp2(m_old - m_new)
                w = jnp.exp2(lse - m_new)
                l_new = a * l_ref[qrows, :] + w
                acc_new = a * acc_ref[qrows, :] + w * o
                if g == n_cfg - 1:
                    o_ref[qrows, :] = acc_new / l_new
                else:
                    m_ref[qrows, :] = m_new
                    l_ref[qrows, :] = l_new
                    acc_ref[qrows, :] = acc_new
            return s_next

        lax.fori_loop(0, n_blocks, block, scores(0), unroll=8)


def _dilated_mixture(p, slopes):
    b, t, _ = p.shape
    spec = lambda off: pl.BlockSpec((None, t, HEAD_DIM), lambda i, h: (i, 0, off + h))
    seq = lambda: pltpu.VMEM((t, HEAD_DIM), F32)
    return pl.pallas_call(
        _dilated_mix_body,
        grid=(b, A_HEADS),
        in_specs=[pl.BlockSpec(memory_space=pltpu.SMEM), spec(0), spec(A_HEADS), spec(2 * A_HEADS)],
        out_specs=pl.BlockSpec((None, t, HEAD_DIM), lambda i, h: (i, 0, h)),
        out_shape=jax.ShapeDtypeStruct((b, t, A_W), F32),
        scratch_shapes=[seq(), seq(), seq(), seq(), seq(), seq(),
                        pltpu.VMEM((3 * len(DILATED_CONFIGS), A_QB, A_KB), F32)],
        compiler_params=_params(2),
        name="dilated_mixture",
    )(slopes, p, p, p)


B_BLK = 256
B_DEN_ROWS = 16
B_AHEAD = 2


def _diff_bias_coefs():
    c = (2.0 ** (-8.0 * np.arange(1, B_HEADS + 1) / B_HEADS)).astype(np.float64) * LOG2E
    out = np.zeros((B_HEADS, 4), np.float32)
    rest = c.copy()
    for i in range(3):
        term = rest.astype(np.float32).astype(BF16).astype(np.float64)
        out[:, i] = term
        rest = rest - term
    out[:, 3] = c
    return out


def _diff_body(coef_ref, lam_ref, q_ref, k_ref, v_ref, o_ref, kaug_ref, vt_ref, *, lam_init):
    h = pl.program_id(1)
    qi = pl.program_id(2)
    nkb = k_ref.shape[0] // B_BLK
    c1, c2, c3, c = coef_ref[h, 0], coef_ref[h, 1], coef_ref[h, 2], coef_ref[h, 3]
    lane = lax.broadcasted_iota(jnp.int32, (B_BLK, LANE), 1)
    pos = lax.broadcasted_iota(jnp.int32, (B_BLK, LANE), 0).astype(F32)
    coefs = jnp.where(lane % 3 == 0, c1, jnp.where(lane % 3 == 1, c2, c3))

    @pl.when(qi == 0)
    def _():
        feat = jnp.where(lane < 3, -pos, jnp.where(lane < 6, coefs, 0.0)).astype(BF16)

        def fill(i, carry):
            rows = pl.ds(pl.multiple_of(i * B_BLK, B_BLK), B_BLK)
            kaug_ref[i, :, 0:LANE] = k_ref[rows, :]
            kaug_ref[i, :, LANE:2 * LANE] = feat
            vt_ref[i, 0:HEAD_DIM, :] = v_ref[rows, :].astype(F32).T.astype(BF16)
            vt_ref[i, HEAD_DIM:HEAD_DIM + B_DEN_ROWS, :] = jnp.ones((B_DEN_ROWS, B_BLK), BF16)
            return carry

        lax.fori_loop(0, nkb, fill, 0)

    qfeat = jnp.where(lane < 3, coefs, jnp.where(lane < 6, pos, 0.0))
    qfeat_right = qfeat.astype(BF16)
    qfeat_left = (-qfeat).astype(BF16)
    q = q_ref[...]
    zero = jnp.zeros_like(q)
    q1 = jnp.where(lane < B_QK_DIM, q, zero)
    q2 = jnp.where(lane >= B_QK_DIM, q, zero)
    nt = (((1,), (1,)), ((), ()))

    def scores(kb, qf):
        q_aug = jnp.concatenate([jnp.concatenate([q1, qf], axis=1), jnp.concatenate([q2, qf], axis=1)], axis=0)
        return lax.dot_general(kaug_ref[kb], q_aug, nt, preferred_element_type=F32)

    s = jnp.minimum(scores(qi, qfeat_right), scores(qi, qfeat_left))
    m_run = jnp.max(s, axis=0, keepdims=True)
    p = jnp.exp2(s - m_run)
    acc = jnp.dot(vt_ref[qi], p.astype(BF16), preferred_element_type=F32)

    def off_diagonal(n):
        kb = n + (n >= qi).astype(jnp.int32)
        return kb, scores(kb, jnp.where(n < qi, qfeat_left, qfeat_right))

    pending = [off_diagonal(n) for n in range(B_AHEAD)]
    for n in range(nkb - 1):
        kb, s = pending.pop(0)
        if n + B_AHEAD < nkb - 1:
            pending.append(off_diagonal(n + B_AHEAD))
        shift = (-c * B_BLK) * jnp.abs(kb - qi).astype(F32)
        m_new = jnp.maximum(m_run, jnp.max(s, axis=0, keepdims=True) + shift)
        alpha = jnp.exp2(m_run - m_new)
        p = jnp.exp2(s - (m_new - shift))
        acc = alpha * acc + jnp.dot(vt_ref[kb], p.astype(BF16), preferred_element_type=F32)
        m_run = m_new

    dl = lam_ref[...]
    lam = (jnp.exp(jnp.sum(dl[0:1, :] * dl[1:2, :], axis=-1, keepdims=True))
           - jnp.exp(jnp.sum(dl[2:3, :] * dl[3:4, :], axis=-1, keepdims=True)) + lam_init)
    o = acc[0:HEAD_DIM, :] / acc[HEAD_DIM:HEAD_DIM + 1, :]
    o_ref[...] = (o[:, 0:B_BLK] - lam * o[:, B_BLK:2 * B_BLK]).T


def _diff_attention(p, diff_lambda, lam_init):
    b, t, _ = p.shape
    nkb = t // B_BLK
    qoff, koff, voff = P_BQ // HEAD_DIM, P_BK // HEAD_DIM, P_BV // HEAD_DIM
    return pl.pallas_call(
        functools.partial(_diff_body, lam_init=lam_init),
        grid=(b, B_HEADS, nkb),
        in_specs=[
            pl.BlockSpec(memory_space=pltpu.SMEM),
            pl.BlockSpec((4, B_QK_DIM), lambda i, h, qi: (0, 0)),
            pl.BlockSpec((None, B_BLK, HEAD_DIM), lambda i, h, qi: (i, qi, qoff + h)),
            pl.BlockSpec((None, t, HEAD_DIM), lambda i, h, qi: (i, 0, koff + h)),
            pl.BlockSpec((None, t, HEAD_DIM), lambda i, h, qi: (i, 0, voff + h)),
        ],
        out_specs=pl.BlockSpec((None, B_BLK, HEAD_DIM), lambda i, h, qi: (i, qi, h)),
        out_shape=jax.ShapeDtypeStruct((b, t, B_V_W), F32),
        scratch_shapes=[pltpu.VMEM((nkb, B_BLK, 2 * LANE), BF16), pltpu.VMEM((nkb, HEAD_DIM + B_DEN_ROWS, B_BLK), BF16)],
        compiler_params=_params(3),
        name="diff_attention",
    )(jnp.asarray(_diff_bias_coefs()), diff_lambda, p, p, p)


def _log_sigmoid(x):
    return jnp.minimum(x, 0.0) - jnp.log(1.0 + jnp.exp(-jnp.abs(x)))


C_HEADS_PER_STEP = 3


def _mlstm_body(q_ref, k_ref, v_ref, og_ref, gcol_ref, grow_ref, bcol_ref, brow_ref, o_ref, c_ref):
    lc = MLSTM_CHUNK
    t = q_ref.shape[0]
    nc = t // lc
    heads = range(C_HEADS_PER_STEP)
    row = lax.broadcasted_iota(jnp.int32, (lc, lc), 0)
    col = lax.broadcasted_iota(jnp.int32, (lc, lc), 1)
    lower = col <= row
    upper = col >= row
    tri_lower = lower.astype(F32)
    tri_upper = upper.astype(F32)
    ones_col = (lax.broadcasted_iota(jnp.int32, (lc, LANE), 1) == 0).astype(BF16)
    c_ref[...] = jnp.zeros(c_ref.shape, F32)

    nt = (((1,), (1,)), ((), ()))

    def make_step(first_touch):
        def step(i, carry):
            scans = [(hh, d, ci) for hh in heads for d, ci in ((0, i), (1, nc - 1 - i))]
            ns = len(scans)
            qs, ks, vs, s_qk, s_qc, gates = [], [], [], [], [], []
            for hh, d, ci in scans:
                rows = pl.ds(pl.multiple_of(ci * lc, lc), lc)
                cols = slice(hh * HEAD_DIM, (hh + 1) * HEAD_DIM)
                q, k, v = q_ref[rows, cols], k_ref[rows, cols], v_ref[rows, cols]
                qs.append(q), ks.append(k), vs.append(v)
                s_qk.append(lax.dot_general(q, k, nt, preferred_element_type=F32))
                s_qc.append(jnp.dot(q, c_ref[2 * hh + d].astype(BF16), preferred_element_type=F32))

            for hh, d, ci in scans:
                cum_col, cum_row = (tri_lower, tri_upper) if d == 0 else (tri_upper, tri_lower)
                rows = pl.ds(pl.multiple_of(ci * lc, lc), lc)
                gc = gcol_ref[rows, :] + bcol_ref[...]
                gr = grow_ref[hh, ci] + brow_ref[hh]
                li_c = gc[:, 4 * hh + 2 * d:4 * hh + 2 * d + 1]
                lf_c = _log_sigmoid(gc[:, 4 * hh + 2 * d + 1:4 * hh + 2 * d + 2])
                li_r = gr[2 * d:2 * d + 1, :]
                lf_r = _log_sigmoid(gr[2 * d + 1:2 * d + 2, :])
                b_c = jnp.sum(cum_col * lf_r, axis=1, keepdims=True)
                b_r = jnp.sum(cum_row * lf_c, axis=0, keepdims=True)
                b_last = jnp.sum(lf_r, axis=-1, keepdims=True)
                gates.append((li_c, li_r, b_c, b_r, b_last))

            scs, m_ts, iws = [], [], []
            for n, (hh, d, ci) in enumerate(scans):
                li_c, li_r, b_c, b_r, b_last = gates[n]
                dmat = jnp.where(lower if d == 0 else upper, b_c - b_r + li_r, NEG)
                inter = b_c + carry[n]
                m_t = jnp.maximum(inter, jnp.max(dmat, axis=-1, keepdims=True))
                scs.append(s_qk[n] * jnp.exp(dmat - m_t))
                m_ts.append(m_t), iws.append(jnp.exp(inter - m_t))

            s_sv = [jnp.dot(scs[n].astype(BF16), vs[n], preferred_element_type=F32) for n in range(ns)]

            new_carry, kg_ts, decays = [], [], []
            for n, (hh, d, ci) in enumerate(scans):
                li_c, li_r, b_c, b_r, b_last = gates[n]
                num = iws[n] * s_qc[n][:, 0:HEAD_DIM] + s_sv[n]
                den = iws[n] * s_qc[n][:, HEAD_DIM:HEAD_DIM + 1] + jnp.sum(scs[n], axis=-1, keepdims=True)
                h_new = num / jnp.maximum(jnp.abs(den), jnp.exp(-m_ts[n]))
                rows = pl.ds(pl.multiple_of(ci * lc, lc), lc)
                cols = slice(hh * HEAD_DIM, (hh + 1) * HEAD_DIM)
                if first_touch:
                    o_ref[rows, cols] = h_new
                else:
                    o_ref[rows, cols] = (o_ref[rows, cols] + h_new) * jax.nn.sigmoid(og_ref[rows, cols])

                g_c = b_last - b_c + li_c
                g_r = b_last - b_r + li_r
                m_new = jnp.maximum(b_last + carry[n], jnp.max(g_r, axis=-1, keepdims=True))
                decays.append(jnp.exp(b_last + carry[n] - m_new))
                kg_ts.append((ks[n].astype(F32) * jnp.exp(g_c - m_new)).T.astype(BF16))
                new_carry.append(m_new)

            for n, (hh, d, ci) in enumerate(scans):
                v_ext = jnp.concatenate([vs[n], ones_col], axis=1)
                c_ref[2 * hh + d] = decays[n] * c_ref[2 * hh + d] + jnp.dot(kg_ts[n], v_ext, preferred_element_type=F32)
            return tuple(new_carry)
        return step

    zero = jnp.zeros((1, 1), F32)
    carry = lax.fori_loop(0, nc // 2, make_step(True), (zero,) * (2 * C_HEADS_PER_STEP))
    lax.fori_loop(nc // 2, nc, make_step(False), carry)


def _mlstm(cqk, p, pf, gate_b):
    b, t, _ = p.shape
    hb = C_HEADS_PER_STEP
    width = hb * HEAD_DIM
    groups = C_HEADS // hb
    nc = t // MLSTM_CHUNK
    cg = pf[:, :, PF_CG:PF_CG + N_GATES].reshape(b, t, 4, C_HEADS)
    gcol = cg.reshape(b, t, 4, groups, hb).transpose(0, 3, 1, 4, 2).reshape(b, groups, t, 4 * hb)
    grow = cg.reshape(b, nc, MLSTM_CHUNK, 4, C_HEADS).transpose(0, 4, 1, 3, 2)
    gb = gate_b.reshape(4, C_HEADS).T
    bcol = gb.reshape(groups, 1, 4 * hb)
    brow = gb.reshape(C_HEADS, 4, 1)
    voff, ooff = P_CV // width, PF_CO // width
    return pl.pallas_call(
        _mlstm_body,
        grid=(b, groups),
        in_specs=[
            pl.BlockSpec((None, t, width), lambda i, h: (i, 0, h)),
            pl.BlockSpec((None, t, width), lambda i, h: (i, 0, groups + h)),
            pl.BlockSpec((None, t, width), lambda i, h: (i, 0, voff + h)),
            pl.BlockSpec((None, t, width), lambda i, h: (i, 0, ooff + h)),
            pl.BlockSpec((None, None, t, 4 * hb), lambda i, h: (i, h, 0, 0)),
            pl.BlockSpec((None, hb, nc, 4, MLSTM_CHUNK), lambda i, h: (i, h, 0, 0, 0)),
            pl.BlockSpec((None, 1, 4 * hb), lambda i, h: (h, 0, 0)),
            pl.BlockSpec((hb, 4, 1), lambda i, h: (h, 0, 0)),
        ],
        out_specs=pl.BlockSpec((None, t, width), lambda i, h: (i, 0, h)),
        out_shape=jax.ShapeDtypeStruct((b, t, C_W), F32),
        scratch_shapes=[pltpu.VMEM((2 * hb, HEAD_DIM, HEAD_DIM + LANE), F32)],
        compiler_params=_params(2),
        name="mlstm",
    )(cqk, cqk, p, pf, gcol, grow, bcol, brow)


MERGE_ROWS = 32


def _merge_out_body(oa_ref, ob_ref, oc_ref, g_ref, w_ref, res_ref, o_ref, lhs_ref, *, b_scale):
    @pl.when(pl.program_id(1) == 0)
    def _():
        def body(i, carry):
            rows = pl.ds(pl.multiple_of(i * MERGE_ROWS, MERGE_ROWS), MERGE_ROWS)
            head = 0
            for src_ref, n_heads, scale in ((oa_ref, A_HEADS, None), (ob_ref, B_HEADS, b_scale), (oc_ref, C_HEADS, None)):
                for h in range(n_heads):
                    o = src_ref[rows, h * HEAD_DIM:(h + 1) * HEAD_DIM]
                    ms = jnp.mean(o * o, axis=-1, keepdims=True)
                    y = o * lax.rsqrt(ms + EPS) * g_ref[:, head * HEAD_DIM:(head + 1) * HEAD_DIM]
                    if scale is not None:
                        y = y * scale
                    lhs_ref[rows, head * HEAD_DIM:(head + 1) * HEAD_DIM] = y.astype(BF16)
                    head += 1
            return carry

        lax.fori_loop(0, lhs_ref.shape[0] // MERGE_ROWS, body, 0, unroll=2)

    o_ref[...] = res_ref[...] + jnp.dot(lhs_ref[...], w_ref[...], preferred_element_type=F32)


def _merge_out(oa, ob, oc, g, w, res, b_scale, tm, tn):
    m, n = res.shape
    row = lambda width: pl.BlockSpec((tm, width), lambda i, j: (i, 0))
    return pl.pallas_call(
        functools.partial(_merge_out_body, b_scale=b_scale),
        grid=(m // tm, n // tn),
        in_specs=[
            row(A_W), row(B_V_W), row(C_W),
            pl.BlockSpec((1, MIX_W), lambda i, j: (0, 0)),
            _weight_spec(w, (MIX_W, tn), lambda i, j: (0, j)),
            pl.BlockSpec((tm, tn), lambda i, j: (i, j)),
        ],
        out_specs=pl.BlockSpec((tm, tn), lambda i, j: (i, j)),
        out_shape=jax.ShapeDtypeStruct((m, n), F32),
        scratch_shapes=[pltpu.VMEM((tm, MIX_W), BF16)],
        compiler_params=_params(2),
        name="merge_out_proj",
    )(oa.reshape(m, A_W), ob.reshape(m, B_V_W), oc.reshape(m, C_W), g.reshape(1, MIX_W), _weight_arg(w), res)


def _xattn_out_body(q_ref, k_ref, v_ref, w_ref, res_ref, o_ref, lhs_ref):
    @pl.when(pl.program_id(1) == 0)
    def _():
        for h in range(X_HEADS):
            cols = slice(h * X_HEAD_DIM, (h + 1) * X_HEAD_DIM)
            s = lax.dot_general(q_ref[:, cols], k_ref[:, cols], (((1,), (1,)), ((), ())),
                                preferred_element_type=F32) * (X_HEAD_DIM ** -0.5)
            m = jnp.max(s, axis=-1, keepdims=True)
            p = jnp.exp(s - m)
            a = p / jnp.sum(p, axis=-1, keepdims=True)
            lhs_ref[:, cols] = jnp.dot(a.astype(BF16), v_ref[:, cols], preferred_element_type=F32).astype(BF16)

    o_ref[...] = res_ref[...] + jnp.dot(lhs_ref[...], w_ref[...], preferred_element_type=F32)


def _xattn_out(q, kv, w, res, seq, tm, tn):
    m, n = res.shape
    mem_len = kv.shape[0] // (m // seq)
    per_batch = seq // tm
    return pl.pallas_call(
        _xattn_out_body,
        grid=(m // tm, n // tn),
        in_specs=[
            pl.BlockSpec((tm, D_MODEL), lambda i, j: (i, 0)),
            pl.BlockSpec((mem_len, D_MODEL), lambda i, j: (i // per_batch, 0)),
            pl.BlockSpec((mem_len, D_MODEL), lambda i, j: (i // per_batch, 1)),
            _weight_spec(w, (D_MODEL, tn), lambda i, j: (0, j)),
            pl.BlockSpec((tm, tn), lambda i, j: (i, j)),
        ],
        out_specs=pl.BlockSpec((tm, tn), lambda i, j: (i, j)),
        out_shape=jax.ShapeDtypeStruct((m, n), F32),
        scratch_shapes=[pltpu.VMEM((tm, D_MODEL), BF16)],
        compiler_params=_params(2),
        name="xattn_out_proj",
    )(q, kv, kv, _weight_arg(w), res)


def _mlp_body(x_ref, g_ref, wu_ref, wd_ref, fg_ref, o_ref, xn_ref, *, final_norm):
    f = pl.program_id(1)

    @pl.when(f == 0)
    def _():
        _rmsnorm_rows_to(x_ref, g_ref, xn_ref)

        def copy(i, carry):
            rows = pl.ds(pl.multiple_of(i * ROW_CHUNK, ROW_CHUNK), ROW_CHUNK)
            o_ref[rows, :] = x_ref[rows, :]
            return carry

        lax.fori_loop(0, x_ref.shape[0] // ROW_CHUNK, copy, 0, unroll=NORM_UNROLL)

    u = jnp.dot(xn_ref[...], wu_ref[...], preferred_element_type=F32)
    a = jnp.square(jnp.maximum(u, 0.0)).astype(BF16)
    o_ref[...] += jnp.dot(a, wd_ref[...], preferred_element_type=F32)

    if final_norm:
        @pl.when(f == pl.num_programs(1) - 1)
        def _():
            def norm(i, carry):
                rows = pl.ds(pl.multiple_of(i * ROW_CHUNK, ROW_CHUNK), ROW_CHUNK)
                y = o_ref[rows, :]
                ms = jnp.mean(y * y, axis=-1, keepdims=True)
                o_ref[rows, :] = y * lax.rsqrt(ms + EPS) * fg_ref[...]
                return carry

            lax.fori_loop(0, o_ref.shape[0] // ROW_CHUNK, norm, 0, unroll=NORM_UNROLL)


def _mlp(x, g, w_up, w_down, final_g, final_norm, tm, tf):
    m, d = x.shape
    d_ff = _weight_shape(w_up)[1]
    return pl.pallas_call(
        functools.partial(_mlp_body, final_norm=final_norm),
        grid=(m // tm, d_ff // tf),
        in_specs=[
            pl.BlockSpec((tm, d), lambda i, f: (i, 0)),
            pl.BlockSpec((1, d), lambda i, f: (0, 0)),
            _weight_spec(w_up, (d, tf), lambda i, f: (0, f)),
            _weight_spec(w_down, (tf, d), lambda i, f: (f, 0)),
            pl.BlockSpec((1, d), lambda i, f: (0, 0)),
        ],
        out_specs=pl.BlockSpec((tm, d), lambda i, f: (i, 0)),
        out_shape=jax.ShapeDtypeStruct((m, d), F32),
        scratch_shapes=[pltpu.VMEM((tm, d), BF16)],
        compiler_params=_params(2),
        name="mlp",
    )(x, g.reshape(1, d), _weight_arg(w_up), _weight_arg(w_down), final_g.reshape(1, d))


def _split_in_proj(w_in):
    o = np.cumsum((0, A_W, A_W, A_W, B_QK_W, B_QK_W, B_V_W, 2 * C_W, C_W, C_W, N_GATES))
    wp = jnp.concatenate([w_in[:, o[0]:o[6]], w_in[:, o[7]:o[8]]], axis=1)
    pad = jnp.zeros((w_in.shape[0], HEAD_DIM - N_GATES), w_in.dtype)
    wpf = jnp.concatenate([w_in[:, o[6]:o[7]], w_in[:, o[8]:o[10]], pad], axis=1)
    return wp, wpf


def kernel(x, mem, norm_mix_g, w_in, conv_w, gate_b, diff_lambda, head_norm_g, w_out, norm_x_g, norm_mem_g,
           w_xq, w_xkv, w_xo, norm_mlp_g, w_up, w_down, final_norm_g):
    b, t, d = x.shape
    m = b * t
    depth = w_in.shape[0]
    h = x.reshape(m, d)
    mem2 = mem.reshape(-1, d)
    slopes_a = _alibi_slopes(A_HEADS)
    p_scale = np.ones((P_W,), np.float32)
    p_scale[0:A_W] = A_Q_SCALE
    p_scale[P_BQ:P_BK] = B_Q_SCALE
    p_scale = jnp.asarray(p_scale)
    w_in, w_out, w_xq, w_xkv, w_xo, w_up, w_down = [_to_bf16(w) for w in (w_in, w_out, w_xq, w_xkv, w_xo, w_up, w_down)]
    for layer in range(depth):
        lam_init = 0.8 - 0.6 * math.exp(-0.3 * layer)
        wp, wpf = _split_in_proj(w_in[layer])
        p = _norm_matmul(h, norm_mix_g[layer], wp, BF16, 1024, 768, p_scale).reshape(b, t, P_W)
        pf = _norm_matmul(h, norm_mix_g[layer], wpf, F32, 512, PF_W).reshape(b, t, PF_W)
        cqk = _conv_silu(pf, conv_w[layer])
        oa = _dilated_mixture(p, slopes_a)
        ob = _diff_attention(p, diff_lambda[layer], lam_init)
        oc = _mlstm(cqk, p, pf, gate_b[layer])
        h = _merge_out(oa, ob, oc, head_norm_g[layer], (layer, w_out), h, 1.0 - lam_init, 512, d)

        q = _norm_matmul(h, norm_x_g[layer], (layer, w_xq), BF16, 1024, d)
        kv = _norm_matmul(mem2, norm_mem_g[layer], (layer, w_xkv), BF16, 512, 1024)
        h = _xattn_out(q, kv, (layer, w_xo), h, t, 512, d)

        h = _mlp(h, norm_mlp_g[layer], (layer, w_up), (layer, w_down), final_norm_g, layer == depth - 1, 1024, 512)
    return h.reshape(b, t, d)
```

```python
import functools
import math

import numpy as np
import jax
import jax.numpy as jnp
from jax import lax
from jax.experimental import pallas as pl
from jax.experimental.pallas import tpu as pltpu

F32 = jnp.float32
BF16 = jnp.bfloat16

D_MODEL = 2048
HEAD_DIM = 128
A_HEADS = 6
B_HEADS = 4
C_HEADS = 6
B_QK_DIM = 64
DILATED_CONFIGS = ((128, 1), (512, 4), (2048, 16))
HALF_WINDOW = 64
MLSTM_CHUNK = 128
X_HEADS = 4
X_HEAD_DIM = D_MODEL // X_HEADS
EPS = 1e-6
NEG = -1e30

A_W = A_HEADS * HEAD_DIM
B_QK_W = B_HEADS * 2 * B_QK_DIM
B_V_W = B_HEADS * HEAD_DIM
C_W = C_HEADS * HEAD_DIM
N_GATES = 4 * C_HEADS
N_MIX_HEADS = A_HEADS + B_HEADS + C_HEADS
MIX_W = N_MIX_HEADS * HEAD_DIM

P_W = 3 * A_W + 2 * B_QK_W + B_V_W + C_W
P_BQ = 3 * A_W
P_BK = P_BQ + B_QK_W
P_BV = P_BK + B_QK_W
P_CV = P_BV + B_V_W
PF_CO = 2 * C_W
PF_CG = PF_CO + C_W
PF_W = PF_CG + HEAD_DIM

LOG2E = math.log2(math.e)
A_Q_SCALE = HEAD_DIM ** -0.5 * LOG2E
B_Q_SCALE = B_QK_DIM ** -0.5 * LOG2E

LANE = 128
ROW_CHUNK = 16
NORM_UNROLL = 4
NORM_GROUP = 256
CAST_BLOCK_BYTES = 4 * 1024 * 1024
VMEM_LIMIT = 56 * 1024 * 1024


def _params(n_axes, vmem=VMEM_LIMIT):
    return pltpu.CompilerParams(dimension_semantics=("arbitrary",) * n_axes, vmem_limit_bytes=vmem)


def _alibi_slopes(n_heads):
    return jnp.asarray(2.0 ** (-8.0 * np.arange(1, n_heads + 1) / n_heads), dtype=F32)


def _weight_spec(w, block, index):
    if isinstance(w, tuple):
        layer, _ = w
        return pl.BlockSpec((None,) + block, lambda *g: (layer,) + index(*g))
    return pl.BlockSpec(block, index)


def _weight_arg(w):
    return w[1] if isinstance(w, tuple) else w


def _weight_shape(w):
    return w[1].shape[1:] if isinstance(w, tuple) else w.shape


def _cast_body(x_ref, o_ref):
    o_ref[...] = x_ref[...].astype(o_ref.dtype)


def _to_bf16(w):
    depth, k, n = w.shape
    tk = k
    while tk * n * 4 > CAST_BLOCK_BYTES and tk % 16 == 0:
        tk //= 2
    spec = pl.BlockSpec((None, tk, n), lambda l, i: (l, i, 0))
    return pl.pallas_call(
        _cast_body,
        grid=(depth, k // tk),
        in_specs=[spec],
        out_specs=spec,
        out_shape=jax.ShapeDtypeStruct(w.shape, BF16),
        compiler_params=_params(2),
        name="cast_bf16",
    )(w)


def _rmsnorm_group_to(x_ref, g_ref, dst_ref, r0):
    for c0 in range(r0, r0 + NORM_GROUP, ROW_CHUNK):
        x = x_ref[c0:c0 + ROW_CHUNK, :]
        ms = jnp.mean(x * x, axis=-1, keepdims=True)
        dst_ref[c0:c0 + ROW_CHUNK, :] = (x * lax.rsqrt(ms + EPS) * g_ref[...]).astype(BF16)


def _norm_matmul_body(x_ref, g_ref, w_ref, *rest, scaled):
    s_ref = rest[0] if scaled else None
    o_ref, xn_ref = rest[-2:]

    def emit(rows):
        acc = jnp.dot(xn_ref[rows, :], w_ref[...], preferred_element_type=F32)
        if scaled:
            acc = acc * s_ref[...]
        o_ref[rows, :] = acc.astype(o_ref.dtype)

    first = pl.program_id(1) == 0

    @pl.when(first)
    def _():
        for r0 in range(0, x_ref.shape[0], NORM_GROUP):
            _rmsnorm_group_to(x_ref, g_ref, xn_ref, r0)
            emit(slice(r0, r0 + NORM_GROUP))

    @pl.when(jnp.logical_not(first))
    def _():
        emit(slice(None))


def _norm_matmul(x, g, w, out_dtype, tm, tn, col_scale=None):
    m, k = x.shape
    n = _weight_shape(w)[1]
    scaled = col_scale is not None
    in_specs = [
        pl.BlockSpec((tm, k), lambda i, j: (i, 0)),
        pl.BlockSpec((1, k), lambda i, j: (0, 0)),
        _weight_spec(w, (k, tn), lambda i, j: (0, j)),
    ]
    args = [x, g.reshape(1, k), _weight_arg(w)]
    if scaled:
        in_specs.append(pl.BlockSpec((1, tn), lambda i, j: (0, j)))
        args.append(col_scale.reshape(1, n))
    return pl.pallas_call(
        functools.partial(_norm_matmul_body, scaled=scaled),
        grid=(m // tm, n // tn),
        in_specs=in_specs,
        out_specs=pl.BlockSpec((tm, tn), lambda i, j: (i, j)),
        out_shape=jax.ShapeDtypeStruct((m, n), out_dtype),
        scratch_shapes=[pltpu.VMEM((tm, k), BF16)],
        compiler_params=_params(2),
        name="norm_matmul",
    )(*args)


CONV_COLS = 256
CONV_PAD = 8


def _conv_silu_body(x_ref, w_ref, o_ref, pad_ref):
    t = x_ref.shape[0]
    zeros = jnp.zeros((CONV_PAD, CONV_COLS), F32)
    pad_ref[0:CONV_PAD, :] = zeros
    pad_ref[CONV_PAD + t:2 * CONV_PAD + t, :] = zeros
    scale = jnp.where(pl.program_id(1) < C_W // CONV_COLS, HEAD_DIM ** -0.5, 1.0).astype(F32)

    def copy(i, carry):
        r0 = pl.multiple_of(i * LANE, LANE)
        pad_ref[pl.ds(CONV_PAD + r0, LANE), :] = x_ref[pl.ds(r0, LANE), :]
        return carry

    lax.fori_loop(0, t // LANE, copy, 0)
    w = w_ref[...]
    row = lax.broadcasted_iota(jnp.int32, (LANE, CONV_COLS), 0)

    def body(i, carry):
        r0 = pl.multiple_of(i * LANE, LANE)
        cur = pad_ref[pl.ds(CONV_PAD + r0, LANE), :]
        before = pad_ref[pl.ds(r0, CONV_PAD), :]
        after = pad_ref[pl.ds(CONV_PAD + LANE + r0, CONV_PAD), :]
        prev = jnp.where(row == 0, before[CONV_PAD - 1:CONV_PAD, :], pltpu.roll(cur, 1, axis=0))
        nxt = jnp.where(row == LANE - 1, after[0:1, :], pltpu.roll(cur, LANE - 1, axis=0))
        y = prev * w[0:1, :] + cur * w[1:2, :] + nxt * w[2:3, :]
        y = y * jax.nn.sigmoid(y)
        o_ref[pl.ds(r0, LANE), :] = (y * scale).astype(BF16)
        return carry

    lax.fori_loop(0, t // LANE, body, 0)


def _conv_silu(pf, conv_w):
    b, t, _ = pf.shape
    ncol = 2 * C_W // CONV_COLS
    return pl.pallas_call(
        _conv_silu_body,
        grid=(b, ncol),
        in_specs=[
            pl.BlockSpec((None, t, CONV_COLS), lambda i, j: (i, 0, j)),
            pl.BlockSpec((3, CONV_COLS), lambda i, j: (0, j)),
        ],
        out_specs=pl.BlockSpec((None, t, CONV_COLS), lambda i, j: (i, 0, j)),
        out_shape=jax.ShapeDtypeStruct((b, t, 2 * C_W), BF16),
        scratch_shapes=[pltpu.VMEM((t + 2 * CONV_PAD, CONV_COLS), F32)],
        compiler_params=_params(2),
        name="conv_silu",
    )(pf, conv_w)


A_QB = 128
A_KB = A_QB + 2 * HALF_WINDOW


def _dilated_mix_body(slopes_ref, q_ref, k_ref, v_ref, o_ref,
                      qf_ref, kf_ref, vf_ref, m_ref, l_ref, acc_ref, bias_ref):
    t = q_ref.shape[0]
    n_blocks = t // A_QB
    base_slope = slopes_ref[pl.program_id(1)] * LOG2E
    row = lax.broadcasted_iota(jnp.int32, (A_QB, A_KB), 0)
    col = lax.broadcasted_iota(jnp.int32, (A_QB, A_KB), 1)
    n_cfg = len(DILATED_CONFIGS)

    for g, (_, r) in enumerate(DILATED_CONFIGS):
        for variant in range(3):
            delta = jnp.abs(col - row - variant * HALF_WINDOW)
            bias_ref[g * 3 + variant] = jnp.where(delta <= HALF_WINDOW, (-base_slope * float(r)) * delta.astype(F32), NEG)

    def widen(i, carry):
        rows = pl.ds(pl.multiple_of(i * A_QB, A_QB), A_QB)
        qf_ref[rows, :] = q_ref[rows, :].astype(F32)
        kf_ref[rows, :] = k_ref[rows, :].astype(F32)
        vf_ref[rows, :] = v_ref[rows, :].astype(F32)
        return carry

    lax.fori_loop(0, n_blocks, widen, 0)

    for g, (_, r) in enumerate(DILATED_CONFIGS):
        n_sub = t // r
        nqb = n_sub // A_QB

        def rows_of(bi, r=r, n_sub=n_sub, nqb=nqb):
            rho = bi // nqb
            q0 = (bi % nqb) * A_QB
            k0 = jnp.clip(q0 - HALF_WINDOW, 0, n_sub - A_KB)
            if r == 1:
                qrows = pl.ds(pl.multiple_of(q0, A_QB), A_QB)
                krows = pl.ds(pl.multiple_of(k0, HALF_WINDOW), A_KB)
            else:
                qrows = pl.ds(rho + r * q0, A_QB, stride=r)
                krows = pl.ds(rho + r * k0, A_KB, stride=r)
            return qrows, krows, (q0 - k0) // HALF_WINDOW

        def scores(bi, g=g, rows_of=rows_of):
            qrows, krows, variant = rows_of(jnp.minimum(bi, n_blocks - 1))
            q = qf_ref[qrows, :].astype(BF16)
            k = kf_ref[krows, :].astype(BF16)
            s = lax.dot_general(q, k, (((1,), (1,)), ((), ())), preferred_element_type=F32)
            return s + bias_ref[g * 3 + variant]

        def block(bi, s, g=g, rows_of=rows_of, scores=scores):
            s_next = scores(bi + 1)
            qrows, krows, _ = rows_of(bi)
            v = vf_ref[krows, :].astype(BF16)
            m = jnp.max(s, axis=-1, keepdims=True)
            p = jnp.exp2(s - m)
            den = jnp.sum(p, axis=-1, keepdims=True)
            o = jnp.dot(p.astype(BF16), v, preferred_element_type=F32) / den
            lse = jnp.broadcast_to(m + jnp.log2(den), (A_QB, HEAD_DIM))
            if g == 0:
                m_ref[qrows, :] = lse
                l_ref[qrows, :] = jnp.ones((A_QB, HEAD_DIM), F32)
                acc_ref[qrows, :] = o
            else:
                m_old = m_ref[qrows, :]
                m_new = jnp.maximum(m_old, lse)
                a = jnp.exp2(m_old - m_new)
                w = jnp.exp2(lse - m_new)
                l_new = a * l_ref[qrows, :] + w
                acc_new = a * acc_ref[qrows, :] + w * o
                if g == n_cfg - 1:
                    o_ref[qrows, :] = acc_new / l_new
                else:
                    m_ref[qrows, :] = m_new
                    l_ref[qrows, :] = l_new
                    acc_ref[qrows, :] = acc_new
            return s_next

        lax.fori_loop(0, n_blocks, block, scores(0), unroll=8)


def _dilated_mixture(p, slopes):
    b, t, _ = p.shape
    spec = lambda off: pl.BlockSpec((None, t, HEAD_DIM), lambda i, h: (i, 0, off + h))
    seq = lambda: pltpu.VMEM((t, HEAD_DIM), F32)
    return pl.pallas_call(
        _dilated_mix_body,
        grid=(b, A_HEADS),
        in_specs=[pl.BlockSpec(memory_space=pltpu.SMEM), spec(0), spec(A_HEADS), spec(2 * A_HEADS)],
        out_specs=pl.BlockSpec((None, t, HEAD_DIM), lambda i, h: (i, 0, h)),
        out_shape=jax.ShapeDtypeStruct((b, t, A_W), F32),
        scratch_shapes=[seq(), seq(), seq(), seq(), seq(), seq(),
                        pltpu.VMEM((3 * len(DILATED_CONFIGS), A_QB, A_KB), F32)],
        compiler_params=_params(2),
        name="dilated_mixture",
    )(slopes, p, p, p)


B_BLK = 256
B_DEN_ROWS = 16
B_AHEAD = 2


def _diff_bias_coefs():
    c = (2.0 ** (-8.0 * np.arange(1, B_HEADS + 1) / B_HEADS)).astype(np.float64) * LOG2E
    out = np.zeros((B_HEADS, 4), np.float32)
    rest = c.copy()
    for i in range(3):
        term = rest.astype(np.float32).astype(BF16).astype(np.float64)
        out[:, i] = term
        rest = rest - term
    out[:, 3] = c
    return out


def _diff_body(coef_ref, lam_ref, q_ref, k_ref, v_ref, o_ref, kaug_ref, vt_ref, *, lam_init):
    h = pl.program_id(1)
    qi = pl.program_id(2)
    nkb = k_ref.shape[0] // B_BLK
    c1, c2, c3, c = coef_ref[h, 0], coef_ref[h, 1], coef_ref[h, 2], coef_ref[h, 3]
    lane = lax.broadcasted_iota(jnp.int32, (B_BLK, LANE), 1)
    pos = lax.broadcasted_iota(jnp.int32, (B_BLK, LANE), 0).astype(F32)
    coefs = jnp.where(lane % 3 == 0, c1, jnp.where(lane % 3 == 1, c2, c3))

    @pl.when(qi == 0)
    def _():
        feat = jnp.where(lane < 3, -pos, jnp.where(lane < 6, coefs, 0.0)).astype(BF16)

        def fill(i, carry):
            rows = pl.ds(pl.multiple_of(i * B_BLK, B_BLK), B_BLK)
            kaug_ref[i, :, 0:LANE] = k_ref[rows, :]
            kaug_ref[i, :, LANE:2 * LANE] = feat
            vt_ref[i, 0:HEAD_DIM, :] = v_ref[rows, :].astype(F32).T.astype(BF16)
            vt_ref[i, HEAD_DIM:HEAD_DIM + B_DEN_ROWS, :] = jnp.ones((B_DEN_ROWS, B_BLK), BF16)
            return carry

        lax.fori_loop(0, nkb, fill, 0)

    qfeat = jnp.where(lane < 3, coefs, jnp.where(lane < 6, pos, 0.0))
    qfeat_right = qfeat.astype(BF16)
    qfeat_left = (-qfeat).astype(BF16)
    q = q_ref[...]
    zero = jnp.zeros_like(q)
    q1 = jnp.where(lane < B_QK_DIM, q, zero)
    q2 = jnp.where(lane >= B_QK_DIM, q, zero)
    nt = (((1,), (1,)), ((), ()))

    def scores(kb, qf):
        q_aug = jnp.concatenate([jnp.concatenate([q1, qf], axis=1), jnp.concatenate([q2, qf], axis=1)], axis=0)
        return lax.dot_general(kaug_ref[kb], q_aug, nt, preferred_element_type=F32)

    s = jnp.minimum(scores(qi, qfeat_right), scores(qi, qfeat_left))
    m_run = jnp.max(s, axis=0, keepdims=True)
    p = jnp.exp2(s - m_run)
    acc = jnp.dot(vt_ref[qi], p.astype(BF16), preferred_element_type=F32)

    def off_diagonal(n):
        kb = n + (n >= qi).astype(jnp.int32)
        return kb, scores(kb, jnp.where(n < qi, qfeat_left, qfeat_right))

    pending = [off_diagonal(n) for n in range(B_AHEAD)]
    for n in range(nkb - 1):
        kb, s = pending.pop(0)
        if n + B_AHEAD < nkb - 1:
            pending.append(off_diagonal(n + B_AHEAD))
        shift = (-c * B_BLK) * jnp.abs(kb - qi).astype(F32)
        m_new = jnp.maximum(m_run, jnp.max(s, axis=0, keepdims=True) + shift)
        alpha = jnp.exp2(m_run - m_new)
        p = jnp.exp2(s - (m_new - shift))
        acc = alpha * acc + jnp.dot(vt_ref[kb], p.astype(BF16), preferred_element_type=F32)
        m_run = m_new

    dl = lam_ref[...]
    lam = (jnp.exp(jnp.sum(dl[0:1, :] * dl[1:2, :], axis=-1, keepdims=True))
           - jnp.exp(jnp.sum(dl[2:3, :] * dl[3:4, :], axis=-1, keepdims=True)) + lam_init)
    o = acc[0:HEAD_DIM, :] / acc[HEAD_DIM:HEAD_DIM + 1, :]
    o_ref[...] = (o[:, 0:B_BLK] - lam * o[:, B_BLK:2 * B_BLK]).T


def _diff_attention(p, diff_lambda, lam_init):
    b, t, _ = p.shape
    nkb = t // B_BLK
    qoff, koff, voff = P_BQ // HEAD_DIM, P_BK // HEAD_DIM, P_BV // HEAD_DIM
    return pl.pallas_call(
        functools.partial(_diff_body, lam_init=lam_init),
        grid=(b, B_HEADS, nkb),
        in_specs=[
            pl.BlockSpec(memory_space=pltpu.SMEM),
            pl.BlockSpec((4, B_QK_DIM), lambda i, h, qi: (0, 0)),
            pl.BlockSpec((None, B_BLK, HEAD_DIM), lambda i, h, qi: (i, qi, qoff + h)),
            pl.BlockSpec((None, t, HEAD_DIM), lambda i, h, qi: (i, 0, koff + h)),
            pl.BlockSpec((None, t, HEAD_DIM), lambda i, h, qi: (i, 0, voff + h)),
        ],
        out_specs=pl.BlockSpec((None, B_BLK, HEAD_DIM), lambda i, h, qi: (i, qi, h)),
        out_shape=jax.ShapeDtypeStruct((b, t, B_V_W), F32),
        scratch_shapes=[pltpu.VMEM((nkb, B_BLK, 2 * LANE), BF16), pltpu.VMEM((nkb, HEAD_DIM + B_DEN_ROWS, B_BLK), BF16)],
        compiler_params=_params(3),
        name="diff_attention",
    )(jnp.asarray(_diff_bias_coefs()), diff_lambda, p, p, p)


def _log_sigmoid(x):
    return jnp.minimum(x, 0.0) - jnp.log(1.0 + jnp.exp(-jnp.abs(x)))


C_HEADS_PER_STEP = 3


def _mlstm_body(q_ref, k_ref, v_ref, og_ref, gcol_ref, grow_ref, bcol_ref, brow_ref, o_ref, c_ref):
    lc = MLSTM_CHUNK
    t = q_ref.shape[0]
    nc = t // lc
    heads = range(C_HEADS_PER_STEP)
    row = lax.broadcasted_iota(jnp.int32, (lc, lc), 0)
    col = lax.broadcasted_iota(jnp.int32, (lc, lc), 1)
    lower = col <= row
    upper = col >= row
    tri_lower = lower.astype(F32)
    tri_upper = upper.astype(F32)
    ones_col = (lax.broadcasted_iota(jnp.int32, (lc, LANE), 1) == 0).astype(BF16)
    c_ref[...] = jnp.zeros(c_ref.shape, F32)

    nt = (((1,), (1,)), ((), ()))

    def make_step(first_touch):
        def step(i, carry):
            scans = [(hh, d, ci) for hh in heads for d, ci in ((0, i), (1, nc - 1 - i))]
            ns = len(scans)
            qs, ks, vs, s_qk, s_qc, gates = [], [], [], [], [], []
            for hh, d, ci in scans:
                rows = pl.ds(pl.multiple_of(ci * lc, lc), lc)
                cols = slice(hh * HEAD_DIM, (hh + 1) * HEAD_DIM)
                q, k, v = q_ref[rows, cols], k_ref[rows, cols], v_ref[rows, cols]
                qs.append(q), ks.append(k), vs.append(v)
                s_qk.append(lax.dot_general(q, k, nt, preferred_element_type=F32))
                s_qc.append(jnp.dot(q, c_ref[2 * hh + d].astype(BF16), preferred_element_type=F32))

            for hh, d, ci in scans:
                cum_col, cum_row = (tri_lower, tri_upper) if d == 0 else (tri_upper, tri_lower)
                rows = pl.ds(pl.multiple_of(ci * lc, lc), lc)
                gc = gcol_ref[rows, :] + bcol_ref[...]
                gr = grow_ref[hh, ci] + brow_ref[hh]
                li_c = gc[:, 4 * hh + 2 * d:4 * hh + 2 * d + 1]
                lf_c = _log_sigmoid(gc[:, 4 * hh + 2 * d + 1:4 * hh + 2 * d + 2])
                li_r = gr[2 * d:2 * d + 1, :]
                lf_r = _log_sigmoid(gr[2 * d + 1:2 * d + 2, :])
                b_c = jnp.sum(cum_col * lf_r, axis=1, keepdims=True)
                b_r = jnp.sum(cum_row * lf_c, axis=0, keepdims=True)
                b_last = jnp.sum(lf_r, axis=-1, keepdims=True)
                gates.append((li_c, li_r, b_c, b_r, b_last))

            scs, m_ts, iws = [], [], []
            for n, (hh, d, ci) in enumerate(scans):
                li_c, li_r, b_c, b_r, b_last = gates[n]
                dmat = jnp.where(lower if d == 0 else upper, b_c - b_r + li_r, NEG)
                inter = b_c + carry[n]
                m_t = jnp.maximum(inter, jnp.max(dmat, axis=-1, keepdims=True))
                scs.append(s_qk[n] * jnp.exp(dmat - m_t))
                m_ts.append(m_t), iws.append(jnp.exp(inter - m_t))

            s_sv = [jnp.dot(scs[n].astype(BF16), vs[n], preferred_element_type=F32) for n in range(ns)]

            new_carry, kg_ts, decays = [], [], []
            for n, (hh, d, ci) in enumerate(scans):
                li_c, li_r, b_c, b_r, b_last = gates[n]
                num = iws[n] * s_qc[n][:, 0:HEAD_DIM] + s_sv[n]
                den = iws[n] * s_qc[n][:, HEAD_DIM:HEAD_DIM + 1] + jnp.sum(scs[n], axis=-1, keepdims=True)
                h_new = num / jnp.maximum(jnp.abs(den), jnp.exp(-m_ts[n]))
                rows = pl.ds(pl.multiple_of(ci * lc, lc), lc)
                cols = slice(hh * HEAD_DIM, (hh + 1) * HEAD_DIM)
                if first_touch:
                    o_ref[rows, cols] = h_new
                else:
                    o_ref[rows, cols] = (o_ref[rows, cols] + h_new) * jax.nn.sigmoid(og_ref[rows, cols])

                g_c = b_last - b_c + li_c
                g_r = b_last - b_r + li_r
                m_new = jnp.maximum(b_last + carry[n], jnp.max(g_r, axis=-1, keepdims=True))
                decays.append(jnp.exp(b_last + carry[n] - m_new))
                kg_ts.append((ks[n].astype(F32) * jnp.exp(g_c - m_new)).T.astype(BF16))
                new_carry.append(m_new)

            for n, (hh, d, ci) in enumerate(scans):
                v_ext = jnp.concatenate([vs[n], ones_col], axis=1)
                c_ref[2 * hh + d] = decays[n] * c_ref[2 * hh + d] + jnp.dot(kg_ts[n], v_ext, preferred_element_type=F32)
            return tuple(new_carry)
        return step

    zero = jnp.zeros((1, 1), F32)
    carry = lax.fori_loop(0, nc // 2, make_step(True), (zero,) * (2 * C_HEADS_PER_STEP))
    lax.fori_loop(nc // 2, nc, make_step(False), carry)


def _mlstm(cqk, p, pf, gate_b):
    b, t, _ = p.shape
    hb = C_HEADS_PER_STEP
    width = hb * HEAD_DIM
    groups = C_HEADS // hb
    nc = t // MLSTM_CHUNK
    cg = pf[:, :, PF_CG:PF_CG + N_GATES].reshape(b, t, 4, C_HEADS)
    gcol = cg.reshape(b, t, 4, groups, hb).transpose(0, 3, 1, 4, 2).reshape(b, groups, t, 4 * hb)
    grow = cg.reshape(b, nc, MLSTM_CHUNK, 4, C_HEADS).transpose(0, 4, 1, 3, 2)
    gb = gate_b.reshape(4, C_HEADS).T
    bcol = gb.reshape(groups, 1, 4 * hb)
    brow = gb.reshape(C_HEADS, 4, 1)
    voff, ooff = P_CV // width, PF_CO // width
    return pl.pallas_call(
        _mlstm_body,
        grid=(b, groups),
        in_specs=[
            pl.BlockSpec((None, t, width), lambda i, h: (i, 0, h)),
            pl.BlockSpec((None, t, width), lambda i, h: (i, 0, groups + h)),
            pl.BlockSpec((None, t, width), lambda i, h: (i, 0, voff + h)),
            pl.BlockSpec((None, t, width), lambda i, h: (i, 0, ooff + h)),
            pl.BlockSpec((None, None, t, 4 * hb), lambda i, h: (i, h, 0, 0)),
            pl.BlockSpec((None, hb, nc, 4, MLSTM_CHUNK), lambda i, h: (i, h, 0, 0, 0)),
            pl.BlockSpec((None, 1, 4 * hb), lambda i, h: (h, 0, 0)),
            pl.BlockSpec((hb, 4, 1), lambda i, h: (h, 0, 0)),
        ],
        out_specs=pl.BlockSpec((None, t, width), lambda i, h: (i, 0, h)),
        out_shape=jax.ShapeDtypeStruct((b, t, C_W), F32),
        scratch_shapes=[pltpu.VMEM((2 * hb, HEAD_DIM, HEAD_DIM + LANE), F32)],
        compiler_params=_params(2),
        name="mlstm",
    )(cqk, cqk, p, pf, gcol, grow, bcol, brow)


MERGE_ROWS = 32


def _merge_out_body(oa_ref, ob_ref, oc_ref, g_ref, w_ref, res_ref, o_ref, lhs_ref, *, b_scale):
    for r0 in range(0, lhs_ref.shape[0], NORM_GROUP):
        for c0 in range(r0, r0 + NORM_GROUP, MERGE_ROWS):
            rows = slice(c0, c0 + MERGE_ROWS)
            head = 0
            for src_ref, n_heads, scale in ((oa_ref, A_HEADS, None), (ob_ref, B_HEADS, b_scale), (oc_ref, C_HEADS, None)):
                for h in range(n_heads):
                    o = src_ref[rows, h * HEAD_DIM:(h + 1) * HEAD_DIM]
                    ms = jnp.mean(o * o, axis=-1, keepdims=True)
                    y = o * lax.rsqrt(ms + EPS) * g_ref[:, head * HEAD_DIM:(head + 1) * HEAD_DIM]
                    if scale is not None:
                        y = y * scale
                    lhs_ref[rows, head * HEAD_DIM:(head + 1) * HEAD_DIM] = y.astype(BF16)
                    head += 1
        group = slice(r0, r0 + NORM_GROUP)
        o_ref[group, :] = res_ref[group, :] + jnp.dot(lhs_ref[group, :], w_ref[...], preferred_element_type=F32)


def _merge_out(oa, ob, oc, g, w, res, b_scale, tm, tn):
    m, n = res.shape
    assert tn == n, "the body writes whole output rows per row group"
    row = lambda width: pl.BlockSpec((tm, width), lambda i, j: (i, 0))
    return pl.pallas_call(
        functools.partial(_merge_out_body, b_scale=b_scale),
        grid=(m // tm, n // tn),
        in_specs=[
            row(A_W), row(B_V_W), row(C_W),
            pl.BlockSpec((1, MIX_W), lambda i, j: (0, 0)),
            _weight_spec(w, (MIX_W, tn), lambda i, j: (0, j)),
            pl.BlockSpec((tm, tn), lambda i, j: (i, j)),
        ],
        out_specs=pl.BlockSpec((tm, tn), lambda i, j: (i, j)),
        out_shape=jax.ShapeDtypeStruct((m, n), F32),
        scratch_shapes=[pltpu.VMEM((tm, MIX_W), BF16)],
        compiler_params=_params(2),
        name="merge_out_proj",
    )(oa.reshape(m, A_W), ob.reshape(m, B_V_W), oc.reshape(m, C_W), g.reshape(1, MIX_W), _weight_arg(w), res)


def _xattn_out_body(q_ref, k_ref, v_ref, w_ref, res_ref, o_ref, lhs_ref):
    for r0 in range(0, lhs_ref.shape[0], NORM_GROUP):
        rows = slice(r0, r0 + NORM_GROUP)
        for h in range(X_HEADS):
            cols = slice(h * X_HEAD_DIM, (h + 1) * X_HEAD_DIM)
            s = lax.dot_general(q_ref[rows, cols], k_ref[:, cols], (((1,), (1,)), ((), ())),
                                preferred_element_type=F32) * (X_HEAD_DIM ** -0.5)
            m = jnp.max(s, axis=-1, keepdims=True)
            p = jnp.exp(s - m)
            a = p / jnp.sum(p, axis=-1, keepdims=True)
            lhs_ref[rows, cols] = jnp.dot(a.astype(BF16), v_ref[:, cols], preferred_element_type=F32).astype(BF16)
        o_ref[rows, :] = res_ref[rows, :] + jnp.dot(lhs_ref[rows, :], w_ref[...], preferred_element_type=F32)


def _xattn_out(q, kv, w, res, seq, tm, tn):
    m, n = res.shape
    assert tn == n, "the body writes whole output rows per row group"
    mem_len = kv.shape[0] // (m // seq)
    per_batch = seq // tm
    return pl.pallas_call(
        _xattn_out_body,
        grid=(m // tm, n // tn),
        in_specs=[
            pl.BlockSpec((tm, D_MODEL), lambda i, j: (i, 0)),
            pl.BlockSpec((mem_len, D_MODEL), lambda i, j: (i // per_batch, 0)),
            pl.BlockSpec((mem_len, D_MODEL), lambda i, j: (i // per_batch, 1)),
            _weight_spec(w, (D_MODEL, tn), lambda i, j: (0, j)),
            pl.BlockSpec((tm, tn), lambda i, j: (i, j)),
        ],
        out_specs=pl.BlockSpec((tm, tn), lambda i, j: (i, j)),
        out_shape=jax.ShapeDtypeStruct((m, n), F32),
        scratch_shapes=[pltpu.VMEM((tm, D_MODEL), BF16)],
        compiler_params=_params(2),
        name="xattn_out_proj",
    )(q, kv, kv, _weight_arg(w), res)


def _mlp_body(x_ref, g_ref, wu_ref, wd_ref, fg_ref, o_ref, xn_ref, *, final_norm):
    f = pl.program_id(1)

    def piece(rows):
        u = jnp.dot(xn_ref[rows, :], wu_ref[...], preferred_element_type=F32)
        a = jnp.square(jnp.maximum(u, 0.0)).astype(BF16)
        return jnp.dot(a, wd_ref[...], preferred_element_type=F32)

    @pl.when(f == 0)
    def _():
        for r0 in range(0, x_ref.shape[0], NORM_GROUP):
            rows = slice(r0, r0 + NORM_GROUP)
            _rmsnorm_group_to(x_ref, g_ref, xn_ref, r0)
            o_ref[rows, :] = x_ref[rows, :] + piece(rows)

    @pl.when(f != 0)
    def _():
        o_ref[...] += piece(slice(None))

    if final_norm:
        @pl.when(f == pl.num_programs(1) - 1)
        def _():
            def norm(i, carry):
                rows = pl.ds(pl.multiple_of(i * ROW_CHUNK, ROW_CHUNK), ROW_CHUNK)
                y = o_ref[rows, :]
                ms = jnp.mean(y * y, axis=-1, keepdims=True)
                o_ref[rows, :] = y * lax.rsqrt(ms + EPS) * fg_ref[...]
                return carry

            lax.fori_loop(0, o_ref.shape[0] // ROW_CHUNK, norm, 0, unroll=NORM_UNROLL)


def _mlp(x, g, w_up, w_down, final_g, final_norm, tm, tf):
    m, d = x.shape
    d_ff = _weight_shape(w_up)[1]
    return pl.pallas_call(
        functools.partial(_mlp_body, final_norm=final_norm),
        grid=(m // tm, d_ff // tf),
        in_specs=[
            pl.BlockSpec((tm, d), lambda i, f: (i, 0)),
            pl.BlockSpec((1, d), lambda i, f: (0, 0)),
            _weight_spec(w_up, (d, tf), lambda i, f: (0, f)),
            _weight_spec(w_down, (tf, d), lambda i, f: (f, 0)),
            pl.BlockSpec((1, d), lambda i, f: (0, 0)),
        ],
        out_specs=pl.BlockSpec((tm, d), lambda i, f: (i, 0)),
        out_shape=jax.ShapeDtypeStruct((m, d), F32),
        scratch_shapes=[pltpu.VMEM((tm, d), BF16)],
        compiler_params=_params(2),
        name="mlp",
    )(x, g.reshape(1, d), _weight_arg(w_up), _weight_arg(w_down), final_g.reshape(1, d))


def _split_in_proj(w_in):
    o = np.cumsum((0, A_W, A_W, A_W, B_QK_W, B_QK_W, B_V_W, 2 * C_W, C_W, C_W, N_GATES))
    wp = jnp.concatenate([w_in[:, o[0]:o[6]], w_in[:, o[7]:o[8]]], axis=1)
    pad = jnp.zeros((w_in.shape[0], HEAD_DIM - N_GATES), w_in.dtype)
    wpf = jnp.concatenate([w_in[:, o[6]:o[7]], w_in[:, o[8]:o[10]], pad], axis=1)
    return wp, wpf


def kernel(x, mem, norm_mix_g, w_in, conv_w, gate_b, diff_lambda, head_norm_g, w_out, norm_x_g, norm_mem_g,
           w_xq, w_xkv, w_xo, norm_mlp_g, w_up, w_down, final_norm_g):
    b, t, d = x.shape
    m = b * t
    depth = w_in.shape[0]
    h = x.reshape(m, d)
    mem2 = mem.reshape(-1, d)
    slopes_a = _alibi_slopes(A_HEADS)
    p_scale = np.ones((P_W,), np.float32)
    p_scale[0:A_W] = A_Q_SCALE
    p_scale[P_BQ:P_BK] = B_Q_SCALE
    p_scale = jnp.asarray(p_scale)
    w_in, w_out, w_xq, w_xkv, w_xo, w_up, w_down = [_to_bf16(w) for w in (w_in, w_out, w_xq, w_xkv, w_xo, w_up, w_down)]
    for layer in range(depth):
        lam_init = 0.8 - 0.6 * math.exp(-0.3 * layer)
        wp, wpf = _split_in_proj(w_in[layer])
        p = _norm_matmul(h, norm_mix_g[layer], wp, BF16, 1024, 768, p_scale).reshape(b, t, P_W)
        pf = _norm_matmul(h, norm_mix_g[layer], wpf, F32, 512, PF_W).reshape(b, t, PF_W)
        cqk = _conv_silu(pf, conv_w[layer])
        oa = _dilated_mixture(p, slopes_a)
        ob = _diff_attention(p, diff_lambda[layer], lam_init)
        oc = _mlstm(cqk, p, pf, gate_b[layer])
        h = _merge_out(oa, ob, oc, head_norm_g[layer], (layer, w_out), h, 1.0 - lam_init, 512, d)

        q = _norm_matmul(h, norm_x_g[layer], (layer, w_xq), BF16, 1024, d)
        kv = _norm_matmul(mem2, norm_mem_g[layer], (layer, w_xkv), BF16, 512, 1024)
        h = _xattn_out(q, kv, (layer, w_xo), h, t, 512, d)

        h = _mlp(h, norm_mlp_g[layer], (layer, w_up), (layer, w_down), final_norm_g, layer == depth - 1, 1024, 512)
    return h.reshape(b, t, d)
```

```python
import functools
import math

import numpy as np
import jax
import jax.numpy as jnp
from jax import lax
from jax.experimental import pallas as pl
from jax.experimental.pallas import tpu as pltpu

F32 = jnp.float32
BF16 = jnp.bfloat16

D_MODEL = 2048
HEAD_DIM = 128
A_HEADS = 6
B_HEADS = 4
C_HEADS = 6
B_QK_DIM = 64
DILATED_CONFIGS = ((128, 1), (512, 4), (2048, 16))
HALF_WINDOW = 64
MLSTM_CHUNK = 128
X_HEADS = 4
X_HEAD_DIM = D_MODEL // X_HEADS
EPS = 1e-6
NEG = -1e30

A_W = A_HEADS * HEAD_DIM
B_QK_W = B_HEADS * 2 * B_QK_DIM
B_V_W = B_HEADS * HEAD_DIM
C_W = C_HEADS * HEAD_DIM
N_GATES = 4 * C_HEADS
N_MIX_HEADS = A_HEADS + B_HEADS + C_HEADS
MIX_W = N_MIX_HEADS * HEAD_DIM

P_W = 3 * A_W + 2 * B_QK_W + B_V_W + C_W
P_BQ = 3 * A_W
P_BK = P_BQ + B_QK_W
P_BV = P_BK + B_QK_W
P_CV = P_BV + B_V_W
PF_CO = 2 * C_W
PF_CG = PF_CO + C_W
PF_W = PF_CG + HEAD_DIM

LOG2E = math.log2(math.e)
A_Q_SCALE = HEAD_DIM ** -0.5 * LOG2E
B_Q_SCALE = B_QK_DIM ** -0.5 * LOG2E

LANE = 128
ROW_CHUNK = 16
NORM_GROUP = 256
CAST_BLOCK_BYTES = 4 * 1024 * 1024
VMEM_LIMIT = 56 * 1024 * 1024


def _params(n_axes, vmem=VMEM_LIMIT):
    return pltpu.CompilerParams(dimension_semantics=("arbitrary",) * n_axes, vmem_limit_bytes=vmem)


def _alibi_slopes(n_heads):
    return jnp.asarray(2.0 ** (-8.0 * np.arange(1, n_heads + 1) / n_heads), dtype=F32)


def _weight_spec(w, block, index):
    if isinstance(w, tuple):
        layer, _ = w
        return pl.BlockSpec((None,) + block, lambda *g: (layer,) + index(*g))
    return pl.BlockSpec(block, index)


def _weight_arg(w):
    return w[1] if isinstance(w, tuple) else w


def _weight_shape(w):
    return w[1].shape[1:] if isinstance(w, tuple) else w.shape


def _cast_body(x_ref, o_ref):
    o_ref[...] = x_ref[...].astype(o_ref.dtype)


def _to_bf16(w):
    depth, k, n = w.shape
    tk = k
    while tk * n * 4 > CAST_BLOCK_BYTES and tk % 16 == 0:
        tk //= 2
    spec = pl.BlockSpec((None, tk, n), lambda l, i: (l, i, 0))
    return pl.pallas_call(
        _cast_body,
        grid=(depth, k // tk),
        in_specs=[spec],
        out_specs=spec,
        out_shape=jax.ShapeDtypeStruct(w.shape, BF16),
        compiler_params=_params(2),
        name="cast_bf16",
    )(w)


def _rmsnorm_group_to(x_ref, g_ref, dst_ref, r0):
    for c0 in range(r0, r0 + NORM_GROUP, ROW_CHUNK):
        x = x_ref[c0:c0 + ROW_CHUNK, :]
        ms = jnp.mean(x * x, axis=-1, keepdims=True)
        dst_ref[c0:c0 + ROW_CHUNK, :] = (x * lax.rsqrt(ms + EPS) * g_ref[...]).astype(BF16)


def _norm_matmul_body(x_ref, g_ref, w_ref, *rest, scaled):
    s_ref = rest[0] if scaled else None
    o_ref, xn_ref = rest[-2:]

    def emit(rows):
        acc = jnp.dot(xn_ref[rows, :], w_ref[...], preferred_element_type=F32)
        if scaled:
            acc = acc * s_ref[...]
        o_ref[rows, :] = acc.astype(o_ref.dtype)

    first = pl.program_id(1) == 0

    @pl.when(first)
    def _():
        for r0 in range(0, x_ref.shape[0], NORM_GROUP):
            _rmsnorm_group_to(x_ref, g_ref, xn_ref, r0)
            emit(slice(r0, r0 + NORM_GROUP))

    @pl.when(jnp.logical_not(first))
    def _():
        emit(slice(None))


def _norm_matmul(x, g, w, out_dtype, tm, tn, col_scale=None):
    m, k = x.shape
    n = _weight_shape(w)[1]
    scaled = col_scale is not None
    in_specs = [
        pl.BlockSpec((tm, k), lambda i, j: (i, 0)),
        pl.BlockSpec((1, k), lambda i, j: (0, 0)),
        _weight_spec(w, (k, tn), lambda i, j: (0, j)),
    ]
    args = [x, g.reshape(1, k), _weight_arg(w)]
    if scaled:
        in_specs.append(pl.BlockSpec((1, tn), lambda i, j: (0, j)))
        args.append(col_scale.reshape(1, n))
    return pl.pallas_call(
        functools.partial(_norm_matmul_body, scaled=scaled),
        grid=(m // tm, n // tn),
        in_specs=in_specs,
        out_specs=pl.BlockSpec((tm, tn), lambda i, j: (i, j)),
        out_shape=jax.ShapeDtypeStruct((m, n), out_dtype),
        scratch_shapes=[pltpu.VMEM((tm, k), BF16)],
        compiler_params=_params(2),
        name="norm_matmul",
    )(*args)


CONV_COLS = 256
CONV_PAD = 8


def _conv_silu_body(x_ref, w_ref, o_ref, pad_ref):
    t = x_ref.shape[0]
    zeros = jnp.zeros((CONV_PAD, CONV_COLS), F32)
    pad_ref[0:CONV_PAD, :] = zeros
    pad_ref[CONV_PAD + t:2 * CONV_PAD + t, :] = zeros
    scale = jnp.where(pl.program_id(1) < C_W // CONV_COLS, HEAD_DIM ** -0.5, 1.0).astype(F32)

    def copy(i, carry):
        r0 = pl.multiple_of(i * LANE, LANE)
        pad_ref[pl.ds(CONV_PAD + r0, LANE), :] = x_ref[pl.ds(r0, LANE), :]
        return carry

    lax.fori_loop(0, t // LANE, copy, 0)
    w = w_ref[...]
    row = lax.broadcasted_iota(jnp.int32, (LANE, CONV_COLS), 0)

    def body(i, carry):
        r0 = pl.multiple_of(i * LANE, LANE)
        cur = pad_ref[pl.ds(CONV_PAD + r0, LANE), :]
        before = pad_ref[pl.ds(r0, CONV_PAD), :]
        after = pad_ref[pl.ds(CONV_PAD + LANE + r0, CONV_PAD), :]
        prev = jnp.where(row == 0, before[CONV_PAD - 1:CONV_PAD, :], pltpu.roll(cur, 1, axis=0))
        nxt = jnp.where(row == LANE - 1, after[0:1, :], pltpu.roll(cur, LANE - 1, axis=0))
        y = prev * w[0:1, :] + cur * w[1:2, :] + nxt * w[2:3, :]
        y = y * jax.nn.sigmoid(y)
        o_ref[pl.ds(r0, LANE), :] = (y * scale).astype(BF16)
        return carry

    lax.fori_loop(0, t // LANE, body, 0)


def _conv_silu(pf, conv_w):
    b, t, _ = pf.shape
    ncol = 2 * C_W // CONV_COLS
    return pl.pallas_call(
        _conv_silu_body,
        grid=(b, ncol),
        in_specs=[
            pl.BlockSpec((None, t, CONV_COLS), lambda i, j: (i, 0, j)),
            pl.BlockSpec((3, CONV_COLS), lambda i, j: (0, j)),
        ],
        out_specs=pl.BlockSpec((None, t, CONV_COLS), lambda i, j: (i, 0, j)),
        out_shape=jax.ShapeDtypeStruct((b, t, 2 * C_W), BF16),
        scratch_shapes=[pltpu.VMEM((t + 2 * CONV_PAD, CONV_COLS), F32)],
        compiler_params=_params(2),
        name="conv_silu",
    )(pf, conv_w)


A_QB = 128
A_KB = A_QB + 2 * HALF_WINDOW


A_SPLIT = 4


def _dilated_mix_body(slopes_ref, q_ref, k_ref, v_ref, o_ref, *scratch):
    n_cfg = len(DILATED_CONFIGS)
    wide0, wide1, fine2 = scratch[0:3], scratch[3:6], scratch[6:9]
    acc_sets = (scratch[9:12], scratch[12:15])
    bias_ref = scratch[15]
    t = q_ref.shape[0]
    n_blocks = t // A_QB
    base_slope = slopes_ref[pl.program_id(1)] * LOG2E
    row = lax.broadcasted_iota(jnp.int32, (A_QB, A_KB), 0)
    col = lax.broadcasted_iota(jnp.int32, (A_QB, A_KB), 1)

    for g, (_, r) in enumerate(DILATED_CONFIGS):
        for variant in range(3):
            delta = jnp.abs(col - row - variant * HALF_WINDOW)
            bias_ref[g * 3 + variant] = jnp.where(delta <= HALF_WINDOW, (-base_slope * float(r)) * delta.astype(F32), NEG)

    def chunk(bi):
        return pl.ds(pl.multiple_of(bi * A_QB, A_QB), A_QB)

    def finer_rows(bi, region):
        r0 = bi * A_QB
        base = (r0 // region) * region
        off = r0 - base
        return pl.ds(base + A_SPLIT * (off % (region // A_SPLIT)) + off // (region // A_SPLIT), A_QB, stride=A_SPLIT)

    def widen(bi, carry):
        for src, dst in zip((q_ref, k_ref, v_ref), wide0):
            dst[chunk(bi), :] = src[chunk(bi), :].astype(F32)
        return carry

    lax.fori_loop(0, n_blocks, widen, 0, unroll=4)

    def split(srcs, dsts, region):
        def body(bi, carry):
            for src, dst in zip(srcs, dsts):
                dst[chunk(bi), :] = src[finer_rows(bi, region), :].astype(dst.dtype)
            return carry
        lax.fori_loop(0, n_blocks, body, 0, unroll=4)

    split(wide0, wide1, t)
    split(wide1, fine2, t // A_SPLIT)
    sources = ((q_ref, k_ref, v_ref), wide1, fine2)

    for g in reversed(range(n_cfg)):
        r = DILATED_CONFIGS[g][1]
        assert r == A_SPLIT ** g
        n_sub = t // r
        nqb = n_sub // A_QB
        qs_ref, ks_ref, vs_ref = sources[g]

        def rows_of(bi, n_sub=n_sub, nqb=nqb):
            base = (bi // nqb) * n_sub
            q0 = (bi % nqb) * A_QB
            k0 = jnp.clip(q0 - HALF_WINDOW, 0, n_sub - A_KB)
            return pl.ds(pl.multiple_of(base + k0, HALF_WINDOW), A_KB), (q0 - k0) // HALF_WINDOW

        def scores(bi, g=g, rows_of=rows_of, qs_ref=qs_ref, ks_ref=ks_ref):
            bi = jnp.minimum(bi, n_blocks - 1)
            krows, variant = rows_of(bi)
            q = qs_ref[chunk(bi), :].astype(BF16)
            k = ks_ref[krows, :].astype(BF16)
            s = lax.dot_general(q, k, (((1,), (1,)), ((), ())), preferred_element_type=F32)
            return s + bias_ref[g * 3 + variant]

        def block(bi, s, g=g, n_sub=n_sub, rows_of=rows_of, scores=scores, vs_ref=vs_ref):
            s_next = scores(bi + 1)
            krows, _ = rows_of(bi)
            v = vs_ref[krows, :].astype(BF16)
            m = jnp.max(s, axis=-1, keepdims=True)
            p = jnp.exp2(s - m)
            den = jnp.sum(p, axis=-1, keepdims=True)
            o = jnp.dot(p.astype(BF16), v, preferred_element_type=F32) / den
            lse = jnp.broadcast_to(m + jnp.log2(den), (A_QB, HEAD_DIM))
            if g == n_cfg - 1:
                m_new, l_new, acc_new = lse, jnp.ones((A_QB, HEAD_DIM), F32), o
            else:
                m_in, l_in, acc_in = acc_sets[g % 2]
                m_old = m_in[chunk(bi), :]
                m_new = jnp.maximum(m_old, lse)
                a = jnp.exp2(m_old - m_new)
                w = jnp.exp2(lse - m_new)
                l_new = a * l_in[chunk(bi), :] + w
                acc_new = a * acc_in[chunk(bi), :] + w * o
            if g == 0:
                o_ref[chunk(bi), :] = acc_new / l_new
            else:
                dst = finer_rows(bi, n_sub * A_SPLIT)
                for ref, val in zip(acc_sets[(g - 1) % 2], (m_new, l_new, acc_new)):
                    ref[dst, :] = val
            return s_next

        lax.fori_loop(0, n_blocks, block, scores(0), unroll=8)


def _dilated_mixture(p, slopes):
    b, t, _ = p.shape
    spec = lambda off: pl.BlockSpec((None, t, HEAD_DIM), lambda i, h: (i, 0, off + h))
    seq = lambda dtype: [pltpu.VMEM((t, HEAD_DIM), dtype) for _ in range(3)]
    return pl.pallas_call(
        _dilated_mix_body,
        grid=(b, A_HEADS),
        in_specs=[pl.BlockSpec(memory_space=pltpu.SMEM), spec(0), spec(A_HEADS), spec(2 * A_HEADS)],
        out_specs=pl.BlockSpec((None, t, HEAD_DIM), lambda i, h: (i, 0, h)),
        out_shape=jax.ShapeDtypeStruct((b, t, A_W), F32),
        scratch_shapes=seq(F32) + seq(F32) + seq(BF16) + seq(F32) + seq(F32)
        + [pltpu.VMEM((3 * len(DILATED_CONFIGS), A_QB, A_KB), F32)],
        compiler_params=_params(2),
        name="dilated_mixture",
    )(slopes, p, p, p)


B_BLK = 256
B_DEN_ROWS = 16
B_AHEAD = 3


def _diff_bias_coefs():
    c = (2.0 ** (-8.0 * np.arange(1, B_HEADS + 1) / B_HEADS)).astype(np.float64) * LOG2E
    out = np.zeros((B_HEADS, 4), np.float32)
    rest = c.copy()
    for i in range(3):
        term = rest.astype(np.float32).astype(BF16).astype(np.float64)
        out[:, i] = term
        rest = rest - term
    out[:, 3] = c
    return out


def _diff_body(coef_ref, lam_ref, q_ref, k_ref, v_ref, o_ref, kaug_ref, vt_ref, *, lam_init):
    h = pl.program_id(1)
    qi = pl.program_id(2)
    nkb = k_ref.shape[0] // B_BLK
    c1, c2, c3, c = coef_ref[h, 0], coef_ref[h, 1], coef_ref[h, 2], coef_ref[h, 3]
    lane = lax.broadcasted_iota(jnp.int32, (B_BLK, LANE), 1)
    pos = lax.broadcasted_iota(jnp.int32, (B_BLK, LANE), 0).astype(F32)
    coefs = jnp.where(lane % 3 == 0, c1, jnp.where(lane % 3 == 1, c2, c3))

    @pl.when(qi == 0)
    def _():
        feat = jnp.where(lane < 3, -pos, jnp.where(lane < 6, coefs, 0.0)).astype(BF16)

        def fill(i, carry):
            rows = pl.ds(pl.multiple_of(i * B_BLK, B_BLK), B_BLK)
            kaug_ref[i, :, 0:LANE] = k_ref[rows, :]
            kaug_ref[i, :, LANE:2 * LANE] = feat
            vt_ref[i, 0:HEAD_DIM, :] = v_ref[rows, :].astype(F32).T.astype(BF16)
            vt_ref[i, HEAD_DIM:HEAD_DIM + B_DEN_ROWS, :] = jnp.ones((B_DEN_ROWS, B_BLK), BF16)
            return carry

        lax.fori_loop(0, nkb, fill, 0)

    qfeat = jnp.where(lane < 3, coefs, jnp.where(lane < 6, pos, 0.0))
    qfeat_right = qfeat.astype(BF16)
    qfeat_left = (-qfeat).astype(BF16)
    q = q_ref[...]
    zero = jnp.zeros_like(q)
    q1 = jnp.where(lane < B_QK_DIM, q, zero)
    q2 = jnp.where(lane >= B_QK_DIM, q, zero)
    nt = (((1,), (1,)), ((), ()))

    def scores(kb, qf):
        q_aug = jnp.concatenate([jnp.concatenate([q1, qf], axis=1), jnp.concatenate([q2, qf], axis=1)], axis=0)
        return lax.dot_general(kaug_ref[kb], q_aug, nt, preferred_element_type=F32)

    s = jnp.minimum(scores(qi, qfeat_right), scores(qi, qfeat_left))
    m_run = jnp.max(s, axis=0, keepdims=True)
    p = jnp.exp2(s - m_run)
    acc = jnp.dot(vt_ref[qi], p.astype(BF16), preferred_element_type=F32)

    def off_diagonal(n):
        kb = n + (n >= qi).astype(jnp.int32)
        return kb, scores(kb, jnp.where(n < qi, qfeat_left, qfeat_right))

    pending = [off_diagonal(n) for n in range(B_AHEAD)]
    for n in range(nkb - 1):
        kb, s = pending.pop(0)
        if n + B_AHEAD < nkb - 1:
            pending.append(off_diagonal(n + B_AHEAD))
        shift = (-c * B_BLK) * jnp.abs(kb - qi).astype(F32)
        m_new = jnp.maximum(m_run, jnp.max(s, axis=0, keepdims=True) + shift)
        alpha = jnp.exp2(m_run - m_new)
        p = jnp.exp2(s - (m_new - shift))
        acc = alpha * acc + jnp.dot(vt_ref[kb], p.astype(BF16), preferred_element_type=F32)
        m_run = m_new

    dl = lam_ref[...]
    lam = (jnp.exp(jnp.sum(dl[0:1, :] * dl[1:2, :], axis=-1, keepdims=True))
           - jnp.exp(jnp.sum(dl[2:3, :] * dl[3:4, :], axis=-1, keepdims=True)) + lam_init)
    o = acc[0:HEAD_DIM, :] / acc[HEAD_DIM:HEAD_DIM + 1, :]
    o_ref[...] = (o[:, 0:B_BLK] - lam * o[:, B_BLK:2 * B_BLK]).T


def _diff_attention(p, diff_lambda, lam_init):
    b, t, _ = p.shape
    nkb = t // B_BLK
    qoff, koff, voff = P_BQ // HEAD_DIM, P_BK // HEAD_DIM, P_BV // HEAD_DIM
    return pl.pallas_call(
        functools.partial(_diff_body, lam_init=lam_init),
        grid=(b, B_HEADS, nkb),
        in_specs=[
            pl.BlockSpec(memory_space=pltpu.SMEM),
            pl.BlockSpec((4, B_QK_DIM), lambda i, h, qi: (0, 0)),
            pl.BlockSpec((None, B_BLK, HEAD_DIM), lambda i, h, qi: (i, qi, qoff + h)),
            pl.BlockSpec((None, t, HEAD_DIM), lambda i, h, qi: (i, 0, koff + h)),
            pl.BlockSpec((None, t, HEAD_DIM), lambda i, h, qi: (i, 0, voff + h)),
        ],
        out_specs=pl.BlockSpec((None, B_BLK, HEAD_DIM), lambda i, h, qi: (i, qi, h)),
        out_shape=jax.ShapeDtypeStruct((b, t, B_V_W), F32),
        scratch_shapes=[pltpu.VMEM((nkb, B_BLK, 2 * LANE), BF16), pltpu.VMEM((nkb, HEAD_DIM + B_DEN_ROWS, B_BLK), BF16)],
        compiler_params=_params(3),
        name="diff_attention",
    )(jnp.asarray(_diff_bias_coefs()), diff_lambda, p, p, p)


def _log_sigmoid(x):
    return jnp.minimum(x, 0.0) - jnp.log(1.0 + jnp.exp(-jnp.abs(x)))


C_HEADS_PER_STEP = 3


def _mlstm_body(q_ref, k_ref, v_ref, og_ref, gcol_ref, grow_ref, bcol_ref, brow_ref, o_ref, c_ref):
    lc = MLSTM_CHUNK
    t = q_ref.shape[0]
    nc = t // lc
    heads = range(C_HEADS_PER_STEP)
    row = lax.broadcasted_iota(jnp.int32, (lc, lc), 0)
    col = lax.broadcasted_iota(jnp.int32, (lc, lc), 1)
    lower = col <= row
    upper = col >= row
    tri_lower = lower.astype(F32)
    tri_upper = upper.astype(F32)
    ones_col = (lax.broadcasted_iota(jnp.int32, (lc, LANE), 1) == 0).astype(BF16)
    c_ref[...] = jnp.zeros(c_ref.shape, F32)

    nt = (((1,), (1,)), ((), ()))

    def make_step(first_touch):
        def step(i, carry):
            scans = [(hh, d, ci) for hh in heads for d, ci in ((0, i), (1, nc - 1 - i))]
            ns = len(scans)
            qs, ks, vs, s_qk, s_qc, gates = [], [], [], [], [], []
            for hh, d, ci in scans:
                rows = pl.ds(pl.multiple_of(ci * lc, lc), lc)
                cols = slice(hh * HEAD_DIM, (hh + 1) * HEAD_DIM)
                q, k, v = q_ref[rows, cols], k_ref[rows, cols], v_ref[rows, cols]
                qs.append(q), ks.append(k), vs.append(v)
                s_qk.append(lax.dot_general(q, k, nt, preferred_element_type=F32))
                s_qc.append(jnp.dot(q, c_ref[2 * hh + d].astype(BF16), preferred_element_type=F32))

            for hh, d, ci in scans:
                cum_col, cum_row = (tri_lower, tri_upper) if d == 0 else (tri_upper, tri_lower)
                rows = pl.ds(pl.multiple_of(ci * lc, lc), lc)
                gc = gcol_ref[rows, :] + bcol_ref[...]
                gr = grow_ref[hh, ci] + brow_ref[hh]
                li_c = gc[:, 4 * hh + 2 * d:4 * hh + 2 * d + 1]
                lf_c = _log_sigmoid(gc[:, 4 * hh + 2 * d + 1:4 * hh + 2 * d + 2])
                li_r = gr[2 * d:2 * d + 1, :]
                lf_r = _log_sigmoid(gr[2 * d + 1:2 * d + 2, :])
                b_c = jnp.sum(cum_col * lf_r, axis=1, keepdims=True)
                b_r = jnp.sum(cum_row * lf_c, axis=0, keepdims=True)
                b_last = jnp.sum(lf_r, axis=-1, keepdims=True)
                gates.append((li_c, li_r, b_c, b_r, b_last))

            scs, m_ts, iws = [], [], []
            for n, (hh, d, ci) in enumerate(scans):
                li_c, li_r, b_c, b_r, b_last = gates[n]
                dmat = jnp.where(lower if d == 0 else upper, b_c - b_r + li_r, NEG)
                inter = b_c + carry[n]
                m_t = jnp.maximum(inter, jnp.max(dmat, axis=-1, keepdims=True))
                scs.append(s_qk[n] * jnp.exp(dmat - m_t))
                m_ts.append(m_t), iws.append(jnp.exp(inter - m_t))

            s_sv = [jnp.dot(scs[n].astype(BF16), vs[n], preferred_element_type=F32) for n in range(ns)]

            new_carry, kg_ts, decays = [], [], []
            for n, (hh, d, ci) in enumerate(scans):
                li_c, li_r, b_c, b_r, b_last = gates[n]
                num = iws[n] * s_qc[n][:, 0:HEAD_DIM] + s_sv[n]
                den = iws[n] * s_qc[n][:, HEAD_DIM:HEAD_DIM + 1] + jnp.sum(scs[n], axis=-1, keepdims=True)
                h_new = num / jnp.maximum(jnp.abs(den), jnp.exp(-m_ts[n]))
                rows = pl.ds(pl.multiple_of(ci * lc, lc), lc)
                cols = slice(hh * HEAD_DIM, (hh + 1) * HEAD_DIM)
                if first_touch:
                    o_ref[rows, cols] = h_new
                else:
                    o_ref[rows, cols] = (o_ref[rows, cols] + h_new) * jax.nn.sigmoid(og_ref[rows, cols])

                g_c = b_last - b_c + li_c
                g_r = b_last - b_r + li_r
                m_new = jnp.maximum(b_last + carry[n], jnp.max(g_r, axis=-1, keepdims=True))
                decays.append(jnp.exp(b_last + carry[n] - m_new))
                kg_ts.append((ks[n].astype(F32) * jnp.exp(g_c - m_new)).T.astype(BF16))
                new_carry.append(m_new)

            for n, (hh, d, ci) in enumerate(scans):
                v_ext = jnp.concatenate([vs[n], ones_col], axis=1)
                c_ref[2 * hh + d] = decays[n] * c_ref[2 * hh + d] + jnp.dot(kg_ts[n], v_ext, preferred_element_type=F32)
            return tuple(new_carry)
        return step

    zero = jnp.zeros((1, 1), F32)
    carry = lax.fori_loop(0, nc // 2, make_step(True), (zero,) * (2 * C_HEADS_PER_STEP))
    lax.fori_loop(nc // 2, nc, make_step(False), carry)


def _mlstm(cqk, p, pf, gate_b):
    b, t, _ = p.shape
    hb = C_HEADS_PER_STEP
    width = hb * HEAD_DIM
    groups = C_HEADS // hb
    nc = t // MLSTM_CHUNK
    cg = pf[:, :, PF_CG:PF_CG + N_GATES].reshape(b, t, 4, C_HEADS)
    gcol = cg.reshape(b, t, 4, groups, hb).transpose(0, 3, 1, 4, 2).reshape(b, groups, t, 4 * hb)
    grow = cg.reshape(b, nc, MLSTM_CHUNK, 4, C_HEADS).transpose(0, 4, 1, 3, 2)
    gb = gate_b.reshape(4, C_HEADS).T
    bcol = gb.reshape(groups, 1, 4 * hb)
    brow = gb.reshape(C_HEADS, 4, 1)
    voff, ooff = P_CV // width, PF_CO // width
    return pl.pallas_call(
        _mlstm_body,
        grid=(b, groups),
        in_specs=[
            pl.BlockSpec((None, t, width), lambda i, h: (i, 0, h)),
            pl.BlockSpec((None, t, width), lambda i, h: (i, 0, groups + h)),
            pl.BlockSpec((None, t, width), lambda i, h: (i, 0, voff + h)),
            pl.BlockSpec((None, t, width), lambda i, h: (i, 0, ooff + h)),
            pl.BlockSpec((None, None, t, 4 * hb), lambda i, h: (i, h, 0, 0)),
            pl.BlockSpec((None, hb, nc, 4, MLSTM_CHUNK), lambda i, h: (i, h, 0, 0, 0)),
            pl.BlockSpec((None, 1, 4 * hb), lambda i, h: (h, 0, 0)),
            pl.BlockSpec((hb, 4, 1), lambda i, h: (h, 0, 0)),
        ],
        out_specs=pl.BlockSpec((None, t, width), lambda i, h: (i, 0, h)),
        out_shape=jax.ShapeDtypeStruct((b, t, C_W), F32),
        scratch_shapes=[pltpu.VMEM((2 * hb, HEAD_DIM, HEAD_DIM + LANE), F32)],
        compiler_params=_params(2),
        name="mlstm",
    )(cqk, cqk, p, pf, gcol, grow, bcol, brow)


MERGE_ROWS = 32


def _merge_out_body(oa_ref, ob_ref, oc_ref, g_ref, w_ref, res_ref, o_ref, lhs_ref, *, b_scale):
    for r0 in range(0, lhs_ref.shape[0], NORM_GROUP):
        for c0 in range(r0, r0 + NORM_GROUP, MERGE_ROWS):
            rows = slice(c0, c0 + MERGE_ROWS)
            head = 0
            for src_ref, n_heads, scale in ((oa_ref, A_HEADS, None), (ob_ref, B_HEADS, b_scale), (oc_ref, C_HEADS, None)):
                for h in range(n_heads):
                    o = src_ref[rows, h * HEAD_DIM:(h + 1) * HEAD_DIM]
                    ms = jnp.mean(o * o, axis=-1, keepdims=True)
                    y = o * lax.rsqrt(ms + EPS) * g_ref[:, head * HEAD_DIM:(head + 1) * HEAD_DIM]
                    if scale is not None:
                        y = y * scale
                    lhs_ref[rows, head * HEAD_DIM:(head + 1) * HEAD_DIM] = y.astype(BF16)
                    head += 1
        group = slice(r0, r0 + NORM_GROUP)
        o_ref[group, :] = res_ref[group, :] + jnp.dot(lhs_ref[group, :], w_ref[...], preferred_element_type=F32)


def _merge_out(oa, ob, oc, g, w, res, b_scale, tm, tn):
    m, n = res.shape
    assert tn == n, "the body writes whole output rows per row group"
    row = lambda width: pl.BlockSpec((tm, width), lambda i, j: (i, 0))
    return pl.pallas_call(
        functools.partial(_merge_out_body, b_scale=b_scale),
        grid=(m // tm, n // tn),
        in_specs=[
            row(A_W), row(B_V_W), row(C_W),
            pl.BlockSpec((1, MIX_W), lambda i, j: (0, 0)),
            _weight_spec(w, (MIX_W, tn), lambda i, j: (0, j)),
            pl.BlockSpec((tm, tn), lambda i, j: (i, j)),
        ],
        out_specs=pl.BlockSpec((tm, tn), lambda i, j: (i, j)),
        out_shape=jax.ShapeDtypeStruct((m, n), F32),
        scratch_shapes=[pltpu.VMEM((tm, MIX_W), BF16)],
        compiler_params=_params(2),
        name="merge_out_proj",
    )(oa.reshape(m, A_W), ob.reshape(m, B_V_W), oc.reshape(m, C_W), g.reshape(1, MIX_W), _weight_arg(w), res)


def _xattn_out_body(q_ref, k_ref, v_ref, w_ref, res_ref, o_ref, lhs_ref):
    for r0 in range(0, lhs_ref.shape[0], NORM_GROUP):
        rows = slice(r0, r0 + NORM_GROUP)
        for h in range(X_HEADS):
            cols = slice(h * X_HEAD_DIM, (h + 1) * X_HEAD_DIM)
            s = lax.dot_general(q_ref[rows, cols], k_ref[:, cols], (((1,), (1,)), ((), ())),
                                preferred_element_type=F32) * (X_HEAD_DIM ** -0.5)
            m = jnp.max(s, axis=-1, keepdims=True)
            p = jnp.exp(s - m)
            a = p / jnp.sum(p, axis=-1, keepdims=True)
            lhs_ref[rows, cols] = jnp.dot(a.astype(BF16), v_ref[:, cols], preferred_element_type=F32).astype(BF16)
        o_ref[rows, :] = res_ref[rows, :] + jnp.dot(lhs_ref[rows, :], w_ref[...], preferred_element_type=F32)


def _xattn_out(q, kv, w, res, seq, tm, tn):
    m, n = res.shape
    assert tn == n, "the body writes whole output rows per row group"
    mem_len = kv.shape[0] // (m // seq)
    per_batch = seq // tm
    return pl.pallas_call(
        _xattn_out_body,
        grid=(m // tm, n // tn),
        in_specs=[
            pl.BlockSpec((tm, D_MODEL), lambda i, j: (i, 0)),
            pl.BlockSpec((mem_len, D_MODEL), lambda i, j: (i // per_batch, 0)),
            pl.BlockSpec((mem_len, D_MODEL), lambda i, j: (i // per_batch, 1)),
            _weight_spec(w, (D_MODEL, tn), lambda i, j: (0, j)),
            pl.BlockSpec((tm, tn), lambda i, j: (i, j)),
        ],
        out_specs=pl.BlockSpec((tm, tn), lambda i, j: (i, j)),
        out_shape=jax.ShapeDtypeStruct((m, n), F32),
        scratch_shapes=[pltpu.VMEM((tm, D_MODEL), BF16)],
        compiler_params=_params(2),
        name="xattn_out_proj",
    )(q, kv, kv, _weight_arg(w), res)


def _mlp_body(x_ref, g_ref, wu_ref, wd_ref, fg_ref, o_ref, xn_ref, *, final_norm):
    f = pl.program_id(1)

    def piece(rows):
        u = jnp.dot(xn_ref[rows, :], wu_ref[...], preferred_element_type=F32)
        a = jnp.square(jnp.maximum(u, 0.0)).astype(BF16)
        return jnp.dot(a, wd_ref[...], preferred_element_type=F32)

    @pl.when(f == 0)
    def _():
        for r0 in range(0, x_ref.shape[0], NORM_GROUP):
            rows = slice(r0, r0 + NORM_GROUP)
            _rmsnorm_group_to(x_ref, g_ref, xn_ref, r0)
            o_ref[rows, :] = x_ref[rows, :] + piece(rows)

    last = pl.num_programs(1) - 1

    @pl.when((f != 0) & ((f != last) | (not final_norm)))
    def _():
        o_ref[...] += piece(slice(None))

    if final_norm:
        @pl.when(f == last)
        def _():
            for r0 in range(0, o_ref.shape[0], NORM_GROUP):
                rows = slice(r0, r0 + NORM_GROUP)
                y = o_ref[rows, :] + piece(rows)
                for c0 in range(0, NORM_GROUP, ROW_CHUNK):
                    yc = y[c0:c0 + ROW_CHUNK, :]
                    ms = jnp.mean(yc * yc, axis=-1, keepdims=True)
                    o_ref[r0 + c0:r0 + c0 + ROW_CHUNK, :] = yc * lax.rsqrt(ms + EPS) * fg_ref[...]


def _mlp(x, g, w_up, w_down, final_g, final_norm, tm, tf):
    m, d = x.shape
    d_ff = _weight_shape(w_up)[1]
    return pl.pallas_call(
        functools.partial(_mlp_body, final_norm=final_norm),
        grid=(m // tm, d_ff // tf),
        in_specs=[
            pl.BlockSpec((tm, d), lambda i, f: (i, 0)),
            pl.BlockSpec((1, d), lambda i, f: (0, 0)),
            _weight_spec(w_up, (d, tf), lambda i, f: (0, f)),
            _weight_spec(w_down, (tf, d), lambda i, f: (f, 0)),
            pl.BlockSpec((1, d), lambda i, f: (0, 0)),
        ],
        out_specs=pl.BlockSpec((tm, d), lambda i, f: (i, 0)),
        out_shape=jax.ShapeDtypeStruct((m, d), F32),
        scratch_shapes=[pltpu.VMEM((tm, d), BF16)],
        compiler_params=_params(2),
        name="mlp",
    )(x, g.reshape(1, d), _weight_arg(w_up), _weight_arg(w_down), final_g.reshape(1, d))


def _split_in_proj(w_in):
    o = np.cumsum((0, A_W, A_W, A_W, B_QK_W, B_QK_W, B_V_W, 2 * C_W, C_W, C_W, N_GATES))
    wp = jnp.concatenate([w_in[:, o[0]:o[6]], w_in[:, o[7]:o[8]]], axis=1)
    pad = jnp.zeros((w_in.shape[0], HEAD_DIM - N_GATES), w_in.dtype)
    wpf = jnp.concatenate([w_in[:, o[6]:o[7]], w_in[:, o[8]:o[10]], pad], axis=1)
    return wp, wpf


def kernel(x, mem, norm_mix_g, w_in, conv_w, gate_b, diff_lambda, head_norm_g, w_out, norm_x_g, norm_mem_g,
           w_xq, w_xkv, w_xo, norm_mlp_g, w_up, w_down, final_norm_g):
    b, t, d = x.shape
    m = b * t
    depth = w_in.shape[0]
    h = x.reshape(m, d)
    mem2 = mem.reshape(-1, d)
    slopes_a = _alibi_slopes(A_HEADS)
    p_scale = np.ones((P_W,), np.float32)
    p_scale[0:A_W] = A_Q_SCALE
    p_scale[P_BQ:P_BK] = B_Q_SCALE
    p_scale = jnp.asarray(p_scale)
    w_out, w_xq, w_xkv, w_xo, w_up, w_down = [_to_bf16(w) for w in (w_out, w_xq, w_xkv, w_xo, w_up, w_down)]
    for layer in range(depth):
        lam_init = 0.8 - 0.6 * math.exp(-0.3 * layer)
        wp, wpf = [w.astype(BF16) for w in _split_in_proj(w_in[layer])]
        p = _norm_matmul(h, norm_mix_g[layer], wp, BF16, 1024, 768, p_scale).reshape(b, t, P_W)
        pf = _norm_matmul(h, norm_mix_g[layer], wpf, F32, 512, PF_W).reshape(b, t, PF_W)
        cqk = _conv_silu(pf, conv_w[layer])
        oa = _dilated_mixture(p, slopes_a)
        ob = _diff_attention(p, diff_lambda[layer], lam_init)
        oc = _mlstm(cqk, p, pf, gate_b[layer])
        h = _merge_out(oa, ob, oc, head_norm_g[layer], (layer, w_out), h, 1.0 - lam_init, 512, d)

        q = _norm_matmul(h, norm_x_g[layer], (layer, w_xq), BF16, 1024, d)
        kv = _norm_matmul(mem2, norm_mem_g[layer], (layer, w_xkv), BF16, 512, 1024)
        h = _xattn_out(q, kv, (layer, w_xo), h, t, 512, d)

        h = _mlp(h, norm_mlp_g[layer], (layer, w_up), (layer, w_down), final_norm_g, layer == depth - 1, 1024, 512)
    return h.reshape(b, t, d)
```

```python
import functools
import math

import numpy as np
import jax
import jax.numpy as jnp
from jax import lax
from jax.experimental import pallas as pl
from jax.experimental.pallas import tpu as pltpu

F32 = jnp.float32
BF16 = jnp.bfloat16

D_MODEL = 2048
HEAD_DIM = 128
A_HEADS = 6
B_HEADS = 4
C_HEADS = 6
B_QK_DIM = 64
DILATED_CONFIGS = ((128, 1), (512, 4), (2048, 16))
HALF_WINDOW = 64
MLSTM_CHUNK = 128
X_HEADS = 4
X_HEAD_DIM = D_MODEL // X_HEADS
EPS = 1e-6
NEG = -1e30

A_W = A_HEADS * HEAD_DIM
B_QK_W = B_HEADS * 2 * B_QK_DIM
B_V_W = B_HEADS * HEAD_DIM
C_W = C_HEADS * HEAD_DIM
N_GATES = 4 * C_HEADS
N_MIX_HEADS = A_HEADS + B_HEADS + C_HEADS
MIX_W = N_MIX_HEADS * HEAD_DIM

P_W = 3 * A_W + 2 * B_QK_W + B_V_W + C_W
P_BQ = 3 * A_W
P_BK = P_BQ + B_QK_W
P_BV = P_BK + B_QK_W
P_CV = P_BV + B_V_W
PF_CO = 2 * C_W
PF_CG = PF_CO + C_W
PF_W = PF_CG + HEAD_DIM

LOG2E = math.log2(math.e)
A_Q_SCALE = HEAD_DIM ** -0.5 * LOG2E
B_Q_SCALE = B_QK_DIM ** -0.5 * LOG2E

LANE = 128
ROW_CHUNK = 16
NORM_GROUP = 256
CAST_BLOCK_BYTES = 2 * 1024 * 1024
VMEM_LIMIT = 56 * 1024 * 1024


def _params(n_axes, vmem=VMEM_LIMIT):
    return pltpu.CompilerParams(dimension_semantics=("arbitrary",) * n_axes, vmem_limit_bytes=vmem)


def _alibi_slopes(n_heads):
    return jnp.asarray(2.0 ** (-8.0 * np.arange(1, n_heads + 1) / n_heads), dtype=F32)


def _weight_spec(w, block, index):
    if isinstance(w, tuple):
        layer, _ = w
        return pl.BlockSpec((None,) + block, lambda *g: (layer,) + index(*g))
    return pl.BlockSpec(block, index)


def _weight_arg(w):
    return w[1] if isinstance(w, tuple) else w


def _weight_shape(w):
    return w[1].shape[1:] if isinstance(w, tuple) else w.shape


def _cast_body(x_ref, o_ref):
    o_ref[...] = x_ref[...].astype(o_ref.dtype)


def _to_bf16(w):
    depth, k, n = w.shape
    tk = k
    while tk * n * 4 > CAST_BLOCK_BYTES and tk % 16 == 0:
        tk //= 2
    spec = pl.BlockSpec((None, tk, n), lambda l, i: (l, i, 0))
    return pl.pallas_call(
        _cast_body,
        grid=(depth, k // tk),
        in_specs=[spec],
        out_specs=spec,
        out_shape=jax.ShapeDtypeStruct(w.shape, BF16),
        compiler_params=_params(2),
        name="cast_bf16",
    )(w)


def _rmsnorm_group_to(x_ref, g_ref, dst_ref, r0):
    for c0 in range(r0, r0 + NORM_GROUP, ROW_CHUNK):
        x = x_ref[c0:c0 + ROW_CHUNK, :]
        ms = jnp.mean(x * x, axis=-1, keepdims=True)
        dst_ref[c0:c0 + ROW_CHUNK, :] = (x * lax.rsqrt(ms + EPS) * g_ref[...]).astype(BF16)


def _norm_matmul_body(x_ref, g_ref, w_ref, *rest, scaled):
    s_ref = rest[0] if scaled else None
    o_ref, xn_ref = rest[-2:]

    def emit(rows):
        acc = jnp.dot(xn_ref[rows, :], w_ref[...], preferred_element_type=F32)
        if scaled:
            acc = acc * s_ref[...]
        o_ref[rows, :] = acc.astype(o_ref.dtype)

    first = pl.program_id(1) == 0

    @pl.when(first)
    def _():
        for r0 in range(0, x_ref.shape[0], NORM_GROUP):
            _rmsnorm_group_to(x_ref, g_ref, xn_ref, r0)
            emit(slice(r0, r0 + NORM_GROUP))

    @pl.when(jnp.logical_not(first))
    def _():
        emit(slice(None))


def _norm_matmul(x, g, w, out_dtype, tm, tn, col_scale=None):
    m, k = x.shape
    n = _weight_shape(w)[1]
    scaled = col_scale is not None
    in_specs = [
        pl.BlockSpec((tm, k), lambda i, j: (i, 0)),
        pl.BlockSpec((1, k), lambda i, j: (0, 0)),
        _weight_spec(w, (k, tn), lambda i, j: (0, j)),
    ]
    args = [x, g.reshape(1, k), _weight_arg(w)]
    if scaled:
        in_specs.append(pl.BlockSpec((1, tn), lambda i, j: (0, j)))
        args.append(col_scale.reshape(1, n))
    return pl.pallas_call(
        functools.partial(_norm_matmul_body, scaled=scaled),
        grid=(m // tm, n // tn),
        in_specs=in_specs,
        out_specs=pl.BlockSpec((tm, tn), lambda i, j: (i, j)),
        out_shape=jax.ShapeDtypeStruct((m, n), out_dtype),
        scratch_shapes=[pltpu.VMEM((tm, k), BF16)],
        compiler_params=_params(2),
        name="norm_matmul",
    )(*args)


CONV_COLS = 256
CONV_PAD = 8


def _conv_silu_body(x_ref, w_ref, o_ref, pad_ref):
    t = x_ref.shape[0]
    zeros = jnp.zeros((CONV_PAD, CONV_COLS), F32)
    pad_ref[0:CONV_PAD, :] = zeros
    pad_ref[CONV_PAD + t:2 * CONV_PAD + t, :] = zeros
    scale = jnp.where(pl.program_id(1) < C_W // CONV_COLS, HEAD_DIM ** -0.5, 1.0).astype(F32)

    def copy(i, carry):
        r0 = pl.multiple_of(i * LANE, LANE)
        pad_ref[pl.ds(CONV_PAD + r0, LANE), :] = x_ref[pl.ds(r0, LANE), :]
        return carry

    lax.fori_loop(0, t // LANE, copy, 0)
    w = w_ref[...]
    row = lax.broadcasted_iota(jnp.int32, (LANE, CONV_COLS), 0)

    def body(i, carry):
        r0 = pl.multiple_of(i * LANE, LANE)
        cur = pad_ref[pl.ds(CONV_PAD + r0, LANE), :]
        before = pad_ref[pl.ds(r0, CONV_PAD), :]
        after = pad_ref[pl.ds(CONV_PAD + LANE + r0, CONV_PAD), :]
        prev = jnp.where(row == 0, before[CONV_PAD - 1:CONV_PAD, :], pltpu.roll(cur, 1, axis=0))
        nxt = jnp.where(row == LANE - 1, after[0:1, :], pltpu.roll(cur, LANE - 1, axis=0))
        y = prev * w[0:1, :] + cur * w[1:2, :] + nxt * w[2:3, :]
        y = y * jax.nn.sigmoid(y)
        o_ref[pl.ds(r0, LANE), :] = (y * scale).astype(BF16)
        return carry

    lax.fori_loop(0, t // LANE, body, 0)


def _conv_silu(pf, conv_w):
    b, t, _ = pf.shape
    ncol = 2 * C_W // CONV_COLS
    return pl.pallas_call(
        _conv_silu_body,
        grid=(b, ncol),
        in_specs=[
            pl.BlockSpec((None, t, CONV_COLS), lambda i, j: (i, 0, j)),
            pl.BlockSpec((3, CONV_COLS), lambda i, j: (0, j)),
        ],
        out_specs=pl.BlockSpec((None, t, CONV_COLS), lambda i, j: (i, 0, j)),
        out_shape=jax.ShapeDtypeStruct((b, t, 2 * C_W), BF16),
        scratch_shapes=[pltpu.VMEM((t + 2 * CONV_PAD, CONV_COLS), F32)],
        compiler_params=_params(2),
        name="conv_silu",
    )(pf, conv_w)


A_QB = 128
A_KB = A_QB + 2 * HALF_WINDOW


A_SPLIT = 4


def _dilated_mix_body(slopes_ref, q_ref, k_ref, v_ref, o_ref, *scratch):
    n_cfg = len(DILATED_CONFIGS)
    wide0, wide1, fine2 = scratch[0:3], scratch[3:6], scratch[6:9]
    acc_sets = (scratch[9:12], scratch[12:15])
    bias_ref = scratch[15]
    t = q_ref.shape[0]
    n_blocks = t // A_QB
    base_slope = slopes_ref[pl.program_id(1)] * LOG2E
    row = lax.broadcasted_iota(jnp.int32, (A_QB, A_KB), 0)
    col = lax.broadcasted_iota(jnp.int32, (A_QB, A_KB), 1)

    for g, (_, r) in enumerate(DILATED_CONFIGS):
        for variant in range(3):
            delta = jnp.abs(col - row - variant * HALF_WINDOW)
            bias_ref[g * 3 + variant] = jnp.where(delta <= HALF_WINDOW, (-base_slope * float(r)) * delta.astype(F32), NEG)

    def chunk(bi):
        return pl.ds(pl.multiple_of(bi * A_QB, A_QB), A_QB)

    def finer_rows(bi, region):
        r0 = bi * A_QB
        base = (r0 // region) * region
        off = r0 - base
        return pl.ds(base + A_SPLIT * (off % (region // A_SPLIT)) + off // (region // A_SPLIT), A_QB, stride=A_SPLIT)

    def widen(bi, carry):
        for src, dst in zip((q_ref, k_ref, v_ref), wide0):
            dst[chunk(bi), :] = src[chunk(bi), :].astype(F32)
        return carry

    lax.fori_loop(0, n_blocks, widen, 0, unroll=4)

    def split(srcs, dsts, region):
        def body(bi, carry):
            for src, dst in zip(srcs, dsts):
                dst[chunk(bi), :] = src[finer_rows(bi, region), :].astype(dst.dtype)
            return carry
        lax.fori_loop(0, n_blocks, body, 0, unroll=4)

    split(wide0, wide1, t)
    split(wide1, fine2, t // A_SPLIT)
    sources = ((q_ref, k_ref, v_ref), wide1, fine2)

    for g in reversed(range(n_cfg)):
        r = DILATED_CONFIGS[g][1]
        assert r == A_SPLIT ** g
        n_sub = t // r
        nqb = n_sub // A_QB
        qs_ref, ks_ref, vs_ref = sources[g]

        def rows_of(bi, n_sub=n_sub, nqb=nqb):
            base = (bi // nqb) * n_sub
            q0 = (bi % nqb) * A_QB
            k0 = jnp.clip(q0 - HALF_WINDOW, 0, n_sub - A_KB)
            return pl.ds(pl.multiple_of(base + k0, HALF_WINDOW), A_KB), (q0 - k0) // HALF_WINDOW

        def scores(bi, g=g, rows_of=rows_of, qs_ref=qs_ref, ks_ref=ks_ref):
            bi = jnp.minimum(bi, n_blocks - 1)
            krows, variant = rows_of(bi)
            q = qs_ref[chunk(bi), :].astype(BF16)
            k = ks_ref[krows, :].astype(BF16)
            s = lax.dot_general(q, k, (((1,), (1,)), ((), ())), preferred_element_type=F32)
            return s + bias_ref[g * 3 + variant]

        def block(bi, s, g=g, n_sub=n_sub, rows_of=rows_of, scores=scores, vs_ref=vs_ref):
            s_next = scores(bi + 1)
            krows, _ = rows_of(bi)
            v = vs_ref[krows, :].astype(BF16)
            m = jnp.max(s, axis=-1, keepdims=True)
            p = jnp.exp2(s - m)
            den = jnp.sum(p, axis=-1, keepdims=True)
            o = jnp.dot(p.astype(BF16), v, preferred_element_type=F32) / den
            lse = jnp.broadcast_to(m + jnp.log2(den), (A_QB, HEAD_DIM))
            if g == n_cfg - 1:
                m_new, l_new, acc_new = lse, jnp.ones((A_QB, HEAD_DIM), F32), o
            else:
                m_in, l_in, acc_in = acc_sets[g % 2]
                m_old = m_in[chunk(bi), :]
                m_new = jnp.maximum(m_old, lse)
                a = jnp.exp2(m_old - m_new)
                w = jnp.exp2(lse - m_new)
                l_new = a * l_in[chunk(bi), :] + w
                acc_new = a * acc_in[chunk(bi), :] + w * o
            if g == 0:
                o_ref[chunk(bi), :] = acc_new / l_new
            else:
                dst = finer_rows(bi, n_sub * A_SPLIT)
                for ref, val in zip(acc_sets[(g - 1) % 2], (m_new, l_new, acc_new)):
                    ref[dst, :] = val
            return s_next

        lax.fori_loop(0, n_blocks, block, scores(0), unroll=8)


def _dilated_mixture(p, slopes):
    b, t, _ = p.shape
    spec = lambda off: pl.BlockSpec((None, t, HEAD_DIM), lambda i, h: (i, 0, off + h))
    seq = lambda dtype: [pltpu.VMEM((t, HEAD_DIM), dtype) for _ in range(3)]
    return pl.pallas_call(
        _dilated_mix_body,
        grid=(b, A_HEADS),
        in_specs=[pl.BlockSpec(memory_space=pltpu.SMEM), spec(0), spec(A_HEADS), spec(2 * A_HEADS)],
        out_specs=pl.BlockSpec((None, t, HEAD_DIM), lambda i, h: (i, 0, h)),
        out_shape=jax.ShapeDtypeStruct((b, t, A_W), F32),
        scratch_shapes=seq(F32) + seq(F32) + seq(BF16) + seq(F32) + seq(F32)
        + [pltpu.VMEM((3 * len(DILATED_CONFIGS), A_QB, A_KB), F32)],
        compiler_params=_params(2),
        name="dilated_mixture",
    )(slopes, p, p, p)


B_BLK = 256
B_DEN_ROWS = 16
B_AHEAD = 4


def _diff_bias_coefs():
    c = (2.0 ** (-8.0 * np.arange(1, B_HEADS + 1) / B_HEADS)).astype(np.float64) * LOG2E
    out = np.zeros((B_HEADS, 4), np.float32)
    rest = c.copy()
    for i in range(3):
        term = rest.astype(np.float32).astype(BF16).astype(np.float64)
        out[:, i] = term
        rest = rest - term
    out[:, 3] = c
    return out


def _diff_body(coef_ref, lam_ref, q_ref, k_ref, v_ref, o_ref, kaug_ref, vt_ref, *, lam_init):
    h = pl.program_id(1)
    qi = pl.program_id(2)
    nkb = k_ref.shape[0] // B_BLK
    c1, c2, c3, c = coef_ref[h, 0], coef_ref[h, 1], coef_ref[h, 2], coef_ref[h, 3]
    lane = lax.broadcasted_iota(jnp.int32, (B_BLK, LANE), 1)
    pos = lax.broadcasted_iota(jnp.int32, (B_BLK, LANE), 0).astype(F32)
    coefs = jnp.where(lane % 3 == 0, c1, jnp.where(lane % 3 == 1, c2, c3))

    @pl.when(qi == 0)
    def _():
        feat = jnp.where(lane < 3, -pos, jnp.where(lane < 6, coefs, 0.0)).astype(BF16)

        def fill(i, carry):
            rows = pl.ds(pl.multiple_of(i * B_BLK, B_BLK), B_BLK)
            kaug_ref[i, :, 0:LANE] = k_ref[rows, :]
            kaug_ref[i, :, LANE:2 * LANE] = feat
            vt_ref[i, 0:HEAD_DIM, :] = v_ref[rows, :].astype(F32).T.astype(BF16)
            vt_ref[i, HEAD_DIM:HEAD_DIM + B_DEN_ROWS, :] = jnp.ones((B_DEN_ROWS, B_BLK), BF16)
            return carry

        lax.fori_loop(0, nkb, fill, 0)

    qfeat = jnp.where(lane < 3, coefs, jnp.where(lane < 6, pos, 0.0))
    qfeat_right = qfeat.astype(BF16)
    qfeat_left = (-qfeat).astype(BF16)
    q = q_ref[...]
    zero = jnp.zeros_like(q)
    q1 = jnp.where(lane < B_QK_DIM, q, zero)
    q2 = jnp.where(lane >= B_QK_DIM, q, zero)
    nt = (((1,), (1,)), ((), ()))

    def scores(kb, qf):
        q_aug = jnp.concatenate([jnp.concatenate([q1, qf], axis=1), jnp.concatenate([q2, qf], axis=1)], axis=0)
        return lax.dot_general(kaug_ref[kb], q_aug, nt, preferred_element_type=F32)

    s = jnp.minimum(scores(qi, qfeat_right), scores(qi, qfeat_left))
    m_run = jnp.max(s, axis=0, keepdims=True)
    p = jnp.exp2(s - m_run)
    acc = jnp.dot(vt_ref[qi], p.astype(BF16), preferred_element_type=F32)

    def off_diagonal(n):
        kb = n + (n >= qi).astype(jnp.int32)
        return kb, scores(kb, jnp.where(n < qi, qfeat_left, qfeat_right))

    pending = [off_diagonal(n) for n in range(B_AHEAD)]
    for n in range(nkb - 1):
        kb, s = pending.pop(0)
        if n + B_AHEAD < nkb - 1:
            pending.append(off_diagonal(n + B_AHEAD))
        shift = (-c * B_BLK) * jnp.abs(kb - qi).astype(F32)
        m_new = jnp.maximum(m_run, jnp.max(s, axis=0, keepdims=True) + shift)
        alpha = jnp.exp2(m_run - m_new)
        p = jnp.exp2(s - (m_new - shift))
        acc = alpha * acc + jnp.dot(vt_ref[kb], p.astype(BF16), preferred_element_type=F32)
        m_run = m_new

    dl = lam_ref[...]
    lam = (jnp.exp(jnp.sum(dl[0:1, :] * dl[1:2, :], axis=-1, keepdims=True))
           - jnp.exp(jnp.sum(dl[2:3, :] * dl[3:4, :], axis=-1, keepdims=True)) + lam_init)
    o = acc[0:HEAD_DIM, :] / acc[HEAD_DIM:HEAD_DIM + 1, :]
    o_ref[...] = (o[:, 0:B_BLK] - lam * o[:, B_BLK:2 * B_BLK]).T


def _diff_attention(p, diff_lambda, lam_init):
    b, t, _ = p.shape
    nkb = t // B_BLK
    qoff, koff, voff = P_BQ // HEAD_DIM, P_BK // HEAD_DIM, P_BV // HEAD_DIM
    return pl.pallas_call(
        functools.partial(_diff_body, lam_init=lam_init),
        grid=(b, B_HEADS, nkb),
        in_specs=[
            pl.BlockSpec(memory_space=pltpu.SMEM),
            pl.BlockSpec((4, B_QK_DIM), lambda i, h, qi: (0, 0)),
            pl.BlockSpec((None, B_BLK, HEAD_DIM), lambda i, h, qi: (i, qi, qoff + h)),
            pl.BlockSpec((None, t, HEAD_DIM), lambda i, h, qi: (i, 0, koff + h)),
            pl.BlockSpec((None, t, HEAD_DIM), lambda i, h, qi: (i, 0, voff + h)),
        ],
        out_specs=pl.BlockSpec((None, B_BLK, HEAD_DIM), lambda i, h, qi: (i, qi, h)),
        out_shape=jax.ShapeDtypeStruct((b, t, B_V_W), F32),
        scratch_shapes=[pltpu.VMEM((nkb, B_BLK, 2 * LANE), BF16), pltpu.VMEM((nkb, HEAD_DIM + B_DEN_ROWS, B_BLK), BF16)],
        compiler_params=_params(3),
        name="diff_attention",
    )(jnp.asarray(_diff_bias_coefs()), diff_lambda, p, p, p)


def _log_sigmoid(x):
    return jnp.minimum(x, 0.0) - jnp.log(1.0 + jnp.exp(-jnp.abs(x)))


C_HEADS_PER_STEP = 3


def _mlstm_body(q_ref, k_ref, v_ref, og_ref, gcol_ref, grow_ref, bcol_ref, brow_ref, o_ref, c_ref):
    lc = MLSTM_CHUNK
    t = q_ref.shape[0]
    nc = t // lc
    heads = range(C_HEADS_PER_STEP)
    row = lax.broadcasted_iota(jnp.int32, (lc, lc), 0)
    col = lax.broadcasted_iota(jnp.int32, (lc, lc), 1)
    lower = col <= row
    upper = col >= row
    tri_lower = lower.astype(F32)
    tri_upper = upper.astype(F32)
    ones_col = (lax.broadcasted_iota(jnp.int32, (lc, LANE), 1) == 0).astype(BF16)
    c_ref[...] = jnp.zeros(c_ref.shape, F32)

    nt = (((1,), (1,)), ((), ()))

    def make_step(first_touch):
        def step(i, carry):
            scans = [(hh, d, ci) for hh in heads for d, ci in ((0, i), (1, nc - 1 - i))]
            ns = len(scans)
            qs, ks, vs, s_qk, s_qc, gates = [], [], [], [], [], []
            for hh, d, ci in scans:
                rows = pl.ds(pl.multiple_of(ci * lc, lc), lc)
                cols = slice(hh * HEAD_DIM, (hh + 1) * HEAD_DIM)
                q, k, v = q_ref[rows, cols], k_ref[rows, cols], v_ref[rows, cols]
                qs.append(q), ks.append(k), vs.append(v)
                s_qk.append(lax.dot_general(q, k, nt, preferred_element_type=F32))
                s_qc.append(jnp.dot(q, c_ref[2 * hh + d].astype(BF16), preferred_element_type=F32))

            for hh, d, ci in scans:
                cum_col, cum_row = (tri_lower, tri_upper) if d == 0 else (tri_upper, tri_lower)
                rows = pl.ds(pl.multiple_of(ci * lc, lc), lc)
                gc = gcol_ref[rows, :] + bcol_ref[...]
                gr = grow_ref[hh, ci] + brow_ref[hh]
                li_c = gc[:, 4 * hh + 2 * d:4 * hh + 2 * d + 1]
                lf_c = _log_sigmoid(gc[:, 4 * hh + 2 * d + 1:4 * hh + 2 * d + 2])
                li_r = gr[2 * d:2 * d + 1, :]
                lf_r = _log_sigmoid(gr[2 * d + 1:2 * d + 2, :])
                b_c = jnp.sum(cum_col * lf_r, axis=1, keepdims=True)
                b_r = jnp.sum(cum_row * lf_c, axis=0, keepdims=True)
                b_last = jnp.sum(lf_r, axis=-1, keepdims=True)
                gates.append((li_c, li_r, b_c, b_r, b_last))

            scs, m_ts, iws = [], [], []
            for n, (hh, d, ci) in enumerate(scans):
                li_c, li_r, b_c, b_r, b_last = gates[n]
                dmat = jnp.where(lower if d == 0 else upper, b_c - b_r + li_r, NEG)
                inter = b_c + carry[n]
                m_t = jnp.maximum(inter, jnp.max(dmat, axis=-1, keepdims=True))
                scs.append(s_qk[n] * jnp.exp(dmat - m_t))
                m_ts.append(m_t), iws.append(jnp.exp(inter - m_t))

            s_sv = [jnp.dot(scs[n].astype(BF16), vs[n], preferred_element_type=F32) for n in range(ns)]

            new_carry, kg_ts, decays = [], [], []
            for n, (hh, d, ci) in enumerate(scans):
                li_c, li_r, b_c, b_r, b_last = gates[n]
                num = iws[n] * s_qc[n][:, 0:HEAD_DIM] + s_sv[n]
                den = iws[n] * s_qc[n][:, HEAD_DIM:HEAD_DIM + 1] + jnp.sum(scs[n], axis=-1, keepdims=True)
                h_new = num / jnp.maximum(jnp.abs(den), jnp.exp(-m_ts[n]))
                rows = pl.ds(pl.multiple_of(ci * lc, lc), lc)
                cols = slice(hh * HEAD_DIM, (hh + 1) * HEAD_DIM)
                if first_touch:
                    o_ref[rows, cols] = h_new
                else:
                    o_ref[rows, cols] = (o_ref[rows, cols] + h_new) * jax.nn.sigmoid(og_ref[rows, cols])

                g_c = b_last - b_c + li_c
                g_r = b_last - b_r + li_r
                m_new = jnp.maximum(b_last + carry[n], jnp.max(g_r, axis=-1, keepdims=True))
                decays.append(jnp.exp(b_last + carry[n] - m_new))
                kg_ts.append((ks[n].astype(F32) * jnp.exp(g_c - m_new)).T.astype(BF16))
                new_carry.append(m_new)

            for n, (hh, d, ci) in enumerate(scans):
                v_ext = jnp.concatenate([vs[n], ones_col], axis=1)
                c_ref[2 * hh + d] = decays[n] * c_ref[2 * hh + d] + jnp.dot(kg_ts[n], v_ext, preferred_element_type=F32)
            return tuple(new_carry)
        return step

    zero = jnp.zeros((1, 1), F32)
    carry = lax.fori_loop(0, nc // 2, make_step(True), (zero,) * (2 * C_HEADS_PER_STEP))
    lax.fori_loop(nc // 2, nc, make_step(False), carry)


def _mlstm(cqk, p, pf, gate_b):
    b, t, _ = p.shape
    hb = C_HEADS_PER_STEP
    width = hb * HEAD_DIM
    groups = C_HEADS // hb
    nc = t // MLSTM_CHUNK
    cg = pf[:, :, PF_CG:PF_CG + N_GATES].reshape(b, t, 4, C_HEADS)
    gcol = cg.reshape(b, t, 4, groups, hb).transpose(0, 3, 1, 4, 2).reshape(b, groups, t, 4 * hb)
    grow = cg.reshape(b, nc, MLSTM_CHUNK, 4, C_HEADS).transpose(0, 4, 1, 3, 2)
    gb = gate_b.reshape(4, C_HEADS).T
    bcol = gb.reshape(groups, 1, 4 * hb)
    brow = gb.reshape(C_HEADS, 4, 1)
    voff, ooff = P_CV // width, PF_CO // width
    return pl.pallas_call(
        _mlstm_body,
        grid=(b, groups),
        in_specs=[
            pl.BlockSpec((None, t, width), lambda i, h: (i, 0, h)),
            pl.BlockSpec((None, t, width), lambda i, h: (i, 0, groups + h)),
            pl.BlockSpec((None, t, width), lambda i, h: (i, 0, voff + h)),
            pl.BlockSpec((None, t, width), lambda i, h: (i, 0, ooff + h)),
            pl.BlockSpec((None, None, t, 4 * hb), lambda i, h: (i, h, 0, 0)),
            pl.BlockSpec((None, hb, nc, 4, MLSTM_CHUNK), lambda i, h: (i, h, 0, 0, 0)),
            pl.BlockSpec((None, 1, 4 * hb), lambda i, h: (h, 0, 0)),
            pl.BlockSpec((hb, 4, 1), lambda i, h: (h, 0, 0)),
        ],
        out_specs=pl.BlockSpec((None, t, width), lambda i, h: (i, 0, h)),
        out_shape=jax.ShapeDtypeStruct((b, t, C_W), F32),
        scratch_shapes=[pltpu.VMEM((2 * hb, HEAD_DIM, HEAD_DIM + LANE), F32)],
        compiler_params=_params(2),
        name="mlstm",
    )(cqk, cqk, p, pf, gcol, grow, bcol, brow)


MERGE_ROWS = 32


def _merge_out_body(oa_ref, ob_ref, oc_ref, g_ref, w_ref, res_ref, o_ref, lhs_ref, *, b_scale):
    for r0 in range(0, lhs_ref.shape[0], NORM_GROUP):
        for c0 in range(r0, r0 + NORM_GROUP, MERGE_ROWS):
            rows = slice(c0, c0 + MERGE_ROWS)
            head = 0
            for src_ref, n_heads, scale in ((oa_ref, A_HEADS, None), (ob_ref, B_HEADS, b_scale), (oc_ref, C_HEADS, None)):
                for h in range(n_heads):
                    o = src_ref[rows, h * HEAD_DIM:(h + 1) * HEAD_DIM]
                    ms = jnp.mean(o * o, axis=-1, keepdims=True)
                    y = o * lax.rsqrt(ms + EPS) * g_ref[:, head * HEAD_DIM:(head + 1) * HEAD_DIM]
                    if scale is not None:
                        y = y * scale
                    lhs_ref[rows, head * HEAD_DIM:(head + 1) * HEAD_DIM] = y.astype(BF16)
                    head += 1
        group = slice(r0, r0 + NORM_GROUP)
        o_ref[group, :] = res_ref[group, :] + jnp.dot(lhs_ref[group, :], w_ref[...], preferred_element_type=F32)


def _merge_out(oa, ob, oc, g, w, res, b_scale, tm, tn):
    m, n = res.shape
    assert tn == n, "the body writes whole output rows per row group"
    row = lambda width: pl.BlockSpec((tm, width), lambda i, j: (i, 0))
    return pl.pallas_call(
        functools.partial(_merge_out_body, b_scale=b_scale),
        grid=(m // tm, n // tn),
        in_specs=[
            row(A_W), row(B_V_W), row(C_W),
            pl.BlockSpec((1, MIX_W), lambda i, j: (0, 0)),
            _weight_spec(w, (MIX_W, tn), lambda i, j: (0, j)),
            pl.BlockSpec((tm, tn), lambda i, j: (i, j)),
        ],
        out_specs=pl.BlockSpec((tm, tn), lambda i, j: (i, j)),
        out_shape=jax.ShapeDtypeStruct((m, n), F32),
        scratch_shapes=[pltpu.VMEM((tm, MIX_W), BF16)],
        compiler_params=_params(2),
        name="merge_out_proj",
    )(oa.reshape(m, A_W), ob.reshape(m, B_V_W), oc.reshape(m, C_W), g.reshape(1, MIX_W), _weight_arg(w), res)


def _xattn_out_body(q_ref, k_ref, v_ref, w_ref, res_ref, o_ref, lhs_ref):
    for r0 in range(0, lhs_ref.shape[0], NORM_GROUP):
        rows = slice(r0, r0 + NORM_GROUP)
        for h in range(X_HEADS):
            cols = slice(h * X_HEAD_DIM, (h + 1) * X_HEAD_DIM)
            s = lax.dot_general(q_ref[rows, cols], k_ref[:, cols], (((1,), (1,)), ((), ())),
                                preferred_element_type=F32) * (X_HEAD_DIM ** -0.5)
            m = jnp.max(s, axis=-1, keepdims=True)
            p = jnp.exp(s - m)
            a = p / jnp.sum(p, axis=-1, keepdims=True)
            lhs_ref[rows, cols] = jnp.dot(a.astype(BF16), v_ref[:, cols], preferred_element_type=F32).astype(BF16)
        o_ref[rows, :] = res_ref[rows, :] + jnp.dot(lhs_ref[rows, :], w_ref[...], preferred_element_type=F32)


def _xattn_out(q, kv, w, res, seq, tm, tn):
    m, n = res.shape
    assert tn == n, "the body writes whole output rows per row group"
    mem_len = kv.shape[0] // (m // seq)
    per_batch = seq // tm
    return pl.pallas_call(
        _xattn_out_body,
        grid=(m // tm, n // tn),
        in_specs=[
            pl.BlockSpec((tm, D_MODEL), lambda i, j: (i, 0)),
            pl.BlockSpec((mem_len, D_MODEL), lambda i, j: (i // per_batch, 0)),
            pl.BlockSpec((mem_len, D_MODEL), lambda i, j: (i // per_batch, 1)),
            _weight_spec(w, (D_MODEL, tn), lambda i, j: (0, j)),
            pl.BlockSpec((tm, tn), lambda i, j: (i, j)),
        ],
        out_specs=pl.BlockSpec((tm, tn), lambda i, j: (i, j)),
        out_shape=jax.ShapeDtypeStruct((m, n), F32),
        scratch_shapes=[pltpu.VMEM((tm, D_MODEL), BF16)],
        compiler_params=_params(2),
        name="xattn_out_proj",
    )(q, kv, kv, _weight_arg(w), res)


def _mlp_body(x_ref, g_ref, wu_ref, wd_ref, fg_ref, o_ref, xn_ref, *, final_norm):
    f = pl.program_id(1)

    def piece(rows):
        u = jnp.dot(xn_ref[rows, :], wu_ref[...], preferred_element_type=F32)
        a = jnp.square(jnp.maximum(u, 0.0)).astype(BF16)
        return jnp.dot(a, wd_ref[...], preferred_element_type=F32)

    @pl.when(f == 0)
    def _():
        for r0 in range(0, x_ref.shape[0], NORM_GROUP):
            rows = slice(r0, r0 + NORM_GROUP)
            _rmsnorm_group_to(x_ref, g_ref, xn_ref, r0)
            o_ref[rows, :] = x_ref[rows, :] + piece(rows)

    last = pl.num_programs(1) - 1

    @pl.when((f != 0) & ((f != last) | (not final_norm)))
    def _():
        o_ref[...] += piece(slice(None))

    if final_norm:
        @pl.when(f == last)
        def _():
            for r0 in range(0, o_ref.shape[0], NORM_GROUP):
                rows = slice(r0, r0 + NORM_GROUP)
                y = o_ref[rows, :] + piece(rows)
                for c0 in range(0, NORM_GROUP, ROW_CHUNK):
                    yc = y[c0:c0 + ROW_CHUNK, :]
                    ms = jnp.mean(yc * yc, axis=-1, keepdims=True)
                    o_ref[r0 + c0:r0 + c0 + ROW_CHUNK, :] = yc * lax.rsqrt(ms + EPS) * fg_ref[...]


def _mlp(x, g, w_up, w_down, final_g, final_norm, tm, tf):
    m, d = x.shape
    d_ff = _weight_shape(w_up)[1]
    return pl.pallas_call(
        functools.partial(_mlp_body, final_norm=final_norm),
        grid=(m // tm, d_ff // tf),
        in_specs=[
            pl.BlockSpec((tm, d), lambda i, f: (i, 0)),
            pl.BlockSpec((1, d), lambda i, f: (0, 0)),
            _weight_spec(w_up, (d, tf), lambda i, f: (0, f)),
            _weight_spec(w_down, (tf, d), lambda i, f: (f, 0)),
            pl.BlockSpec((1, d), lambda i, f: (0, 0)),
        ],
        out_specs=pl.BlockSpec((tm, d), lambda i, f: (i, 0)),
        out_shape=jax.ShapeDtypeStruct((m, d), F32),
        scratch_shapes=[pltpu.VMEM((tm, d), BF16)],
        compiler_params=_params(2),
        name="mlp",
    )(x, g.reshape(1, d), _weight_arg(w_up), _weight_arg(w_down), final_g.reshape(1, d))


def _split_in_proj(w_in):
    o = np.cumsum((0, A_W, A_W, A_W, B_QK_W, B_QK_W, B_V_W, 2 * C_W, C_W, C_W, N_GATES))
    wp = jnp.concatenate([w_in[:, o[0]:o[6]], w_in[:, o[7]:o[8]]], axis=1)
    pad = jnp.zeros((w_in.shape[0], HEAD_DIM - N_GATES), w_in.dtype)
    wpf = jnp.concatenate([w_in[:, o[6]:o[7]], w_in[:, o[8]:o[10]], pad], axis=1)
    return wp, wpf


def kernel(x, mem, norm_mix_g, w_in, conv_w, gate_b, diff_lambda, head_norm_g, w_out, norm_x_g, norm_mem_g,
           w_xq, w_xkv, w_xo, norm_mlp_g, w_up, w_down, final_norm_g):
    b, t, d = x.shape
    m = b * t
    depth = w_in.shape[0]
    h = x.reshape(m, d)
    mem2 = mem.reshape(-1, d)
    slopes_a = _alibi_slopes(A_HEADS)
    p_scale = np.ones((P_W,), np.float32)
    p_scale[0:A_W] = A_Q_SCALE
    p_scale[P_BQ:P_BK] = B_Q_SCALE
    p_scale = jnp.asarray(p_scale)
    w_out, w_xq, w_xkv, w_xo, w_up, w_down = [_to_bf16(w) for w in (w_out, w_xq, w_xkv, w_xo, w_up, w_down)]
    for layer in range(depth):
        lam_init = 0.8 - 0.6 * math.exp(-0.3 * layer)
        wp, wpf = [w.astype(BF16) for w in _split_in_proj(w_in[layer])]
        p = _norm_matmul(h, norm_mix_g[layer], wp, BF16, 1024, 1536, p_scale).reshape(b, t, P_W)
        pf = _norm_matmul(h, norm_mix_g[layer], wpf, F32, 512, PF_W).reshape(b, t, PF_W)
        cqk = _conv_silu(pf, conv_w[layer])
        oa = _dilated_mixture(p, slopes_a)
        ob = _diff_attention(p, diff_lambda[layer], lam_init)
        oc = _mlstm(cqk, p, pf, gate_b[layer])
        h = _merge_out(oa, ob, oc, head_norm_g[layer], (layer, w_out), h, 1.0 - lam_init, 512, d)

        q = _norm_matmul(h, norm_x_g[layer], (layer, w_xq), BF16, 1024, d)
        kv = _norm_matmul(mem2, norm_mem_g[layer], (layer, w_xkv), BF16, 512, 1024)
        h = _xattn_out(q, kv, (layer, w_xo), h, t, 512, d)

        h = _mlp(h, norm_mlp_g[layer], (layer, w_up), (layer, w_down), final_norm_g, layer == depth - 1, 1024, 512)
    return h.reshape(b, t, d)
```

```python
import functools
import math

import numpy as np
import jax
import jax.numpy as jnp
from jax import lax
from jax.experimental import pallas as pl
from jax.experimental.pallas import tpu as pltpu

F32 = jnp.float32
BF16 = jnp.bfloat16

D_MODEL = 2048
HEAD_DIM = 128
A_HEADS = 6
B_HEADS = 4
C_HEADS = 6
B_QK_DIM = 64
DILATED_CONFIGS = ((128, 1), (512, 4), (2048, 16))
HALF_WINDOW = 64
MLSTM_CHUNK = 128
X_HEADS = 4
X_HEAD_DIM = D_MODEL // X_HEADS
EPS = 1e-6
NEG = -1e30

A_W = A_HEADS * HEAD_DIM
B_QK_W = B_HEADS * 2 * B_QK_DIM
B_V_W = B_HEADS * HEAD_DIM
C_W = C_HEADS * HEAD_DIM
N_GATES = 4 * C_HEADS
N_MIX_HEADS = A_HEADS + B_HEADS + C_HEADS
MIX_W = N_MIX_HEADS * HEAD_DIM

P_W = 3 * A_W + 2 * B_QK_W + B_V_W + C_W
P_BQ = 3 * A_W
P_BK = P_BQ + B_QK_W
P_BV = P_BK + B_QK_W
P_CV = P_BV + B_V_W
PF_CO = 2 * C_W
PF_CG = PF_CO + C_W
PF_W = PF_CG + HEAD_DIM

LOG2E = math.log2(math.e)
A_Q_SCALE = HEAD_DIM ** -0.5 * LOG2E
B_Q_SCALE = B_QK_DIM ** -0.5 * LOG2E

LANE = 128
ROW_CHUNK = 16
NORM_GROUP = 256
CAST_BLOCK_BYTES = 8 * 1024 * 1024
VMEM_LIMIT = 56 * 1024 * 1024


def _params(n_axes, vmem=VMEM_LIMIT):
    return pltpu.CompilerParams(dimension_semantics=("arbitrary",) * n_axes, vmem_limit_bytes=vmem)


def _alibi_slopes(n_heads):
    return jnp.asarray(2.0 ** (-8.0 * np.arange(1, n_heads + 1) / n_heads), dtype=F32)


def _weight_spec(w, block, index):
    if isinstance(w, tuple):
        layer, _ = w
        return pl.BlockSpec((None,) + block, lambda *g: (layer,) + index(*g))
    return pl.BlockSpec(block, index)


def _weight_arg(w):
    return w[1] if isinstance(w, tuple) else w


def _weight_shape(w):
    return w[1].shape[1:] if isinstance(w, tuple) else w.shape


def _cast_body(x_ref, o_ref):
    o_ref[...] = x_ref[...].astype(o_ref.dtype)


def _to_bf16(w):
    depth, k, n = w.shape
    tk = k
    while tk * n * 4 > CAST_BLOCK_BYTES and tk % 16 == 0:
        tk //= 2
    spec = pl.BlockSpec((None, tk, n), lambda l, i: (l, i, 0))
    return pl.pallas_call(
        _cast_body,
        grid=(depth, k // tk),
        in_specs=[spec],
        out_specs=spec,
        out_shape=jax.ShapeDtypeStruct(w.shape, BF16),
        compiler_params=_params(2),
        name="cast_bf16",
    )(w)


def _rmsnorm_group_to(x_ref, g_ref, dst_ref, r0):
    for c0 in range(r0, r0 + NORM_GROUP, ROW_CHUNK):
        x = x_ref[c0:c0 + ROW_CHUNK, :]
        ms = jnp.mean(x * x, axis=-1, keepdims=True)
        dst_ref[c0:c0 + ROW_CHUNK, :] = (x * lax.rsqrt(ms + EPS) * g_ref[...]).astype(BF16)


def _norm_matmul_body(x_ref, g_ref, w_ref, *rest, scaled):
    s_ref = rest[0] if scaled else None
    o_ref, xn_ref = rest[-2:]

    def emit(rows):
        acc = jnp.dot(xn_ref[rows, :], w_ref[...], preferred_element_type=F32)
        if scaled:
            acc = acc * s_ref[...]
        o_ref[rows, :] = acc.astype(o_ref.dtype)

    first = pl.program_id(1) == 0

    @pl.when(first)
    def _():
        for r0 in range(0, x_ref.shape[0], NORM_GROUP):
            _rmsnorm_group_to(x_ref, g_ref, xn_ref, r0)
            emit(slice(r0, r0 + NORM_GROUP))

    @pl.when(jnp.logical_not(first))
    def _():
        emit(slice(None))


def _norm_matmul(x, g, w, out_dtype, tm, tn, col_scale=None):
    m, k = x.shape
    n = _weight_shape(w)[1]
    scaled = col_scale is not None
    in_specs = [
        pl.BlockSpec((tm, k), lambda i, j: (i, 0)),
        pl.BlockSpec((1, k), lambda i, j: (0, 0)),
        _weight_spec(w, (k, tn), lambda i, j: (0, j)),
    ]
    args = [x, g.reshape(1, k), _weight_arg(w)]
    if scaled:
        in_specs.append(pl.BlockSpec((1, tn), lambda i, j: (0, j)))
        args.append(col_scale.reshape(1, n))
    return pl.pallas_call(
        functools.partial(_norm_matmul_body, scaled=scaled),
        grid=(m // tm, n // tn),
        in_specs=in_specs,
        out_specs=pl.BlockSpec((tm, tn), lambda i, j: (i, j)),
        out_shape=jax.ShapeDtypeStruct((m, n), out_dtype),
        scratch_shapes=[pltpu.VMEM((tm, k), BF16)],
        compiler_params=_params(2),
        name="norm_matmul",
    )(*args)


CONV_COLS = 256
CONV_PAD = 8


def _conv_silu_body(x_ref, w_ref, o_ref, pad_ref):
    t = x_ref.shape[0]
    zeros = jnp.zeros((CONV_PAD, CONV_COLS), F32)
    pad_ref[0:CONV_PAD, :] = zeros
    pad_ref[CONV_PAD + t:2 * CONV_PAD + t, :] = zeros
    scale = jnp.where(pl.program_id(1) < C_W // CONV_COLS, HEAD_DIM ** -0.5, 1.0).astype(F32)

    def copy(i, carry):
        r0 = pl.multiple_of(i * LANE, LANE)
        pad_ref[pl.ds(CONV_PAD + r0, LANE), :] = x_ref[pl.ds(r0, LANE), :]
        return carry

    lax.fori_loop(0, t // LANE, copy, 0)
    w = w_ref[...]
    row = lax.broadcasted_iota(jnp.int32, (LANE, CONV_COLS), 0)

    def body(i, carry):
        r0 = pl.multiple_of(i * LANE, LANE)
        cur = pad_ref[pl.ds(CONV_PAD + r0, LANE), :]
        before = pad_ref[pl.ds(r0, CONV_PAD), :]
        after = pad_ref[pl.ds(CONV_PAD + LANE + r0, CONV_PAD), :]
        prev = jnp.where(row == 0, before[CONV_PAD - 1:CONV_PAD, :], pltpu.roll(cur, 1, axis=0))
        nxt = jnp.where(row == LANE - 1, after[0:1, :], pltpu.roll(cur, LANE - 1, axis=0))
        y = prev * w[0:1, :] + cur * w[1:2, :] + nxt * w[2:3, :]
        y = y * jax.nn.sigmoid(y)
        o_ref[pl.ds(r0, LANE), :] = (y * scale).astype(BF16)
        return carry

    lax.fori_loop(0, t // LANE, body, 0)


def _conv_silu(pf, conv_w):
    b, t, _ = pf.shape
    ncol = 2 * C_W // CONV_COLS
    return pl.pallas_call(
        _conv_silu_body,
        grid=(b, ncol),
        in_specs=[
            pl.BlockSpec((None, t, CONV_COLS), lambda i, j: (i, 0, j)),
            pl.BlockSpec((3, CONV_COLS), lambda i, j: (0, j)),
        ],
        out_specs=pl.BlockSpec((None, t, CONV_COLS), lambda i, j: (i, 0, j)),
        out_shape=jax.ShapeDtypeStruct((b, t, 2 * C_W), BF16),
        scratch_shapes=[pltpu.VMEM((t + 2 * CONV_PAD, CONV_COLS), F32)],
        compiler_params=_params(2),
        name="conv_silu",
    )(pf, conv_w)


A_QB = 128
A_KB = A_QB + 2 * HALF_WINDOW


A_SPLIT = 4


def _dilated_mix_body(slopes_ref, q_ref, k_ref, v_ref, o_ref, *scratch):
    n_cfg = len(DILATED_CONFIGS)
    wide0, wide1, fine2 = scratch[0:3], scratch[3:6], scratch[6:9]
    acc_sets = (scratch[9:12], scratch[12:15])
    bias_ref = scratch[15]
    t = q_ref.shape[0]
    n_blocks = t // A_QB
    base_slope = slopes_ref[pl.program_id(1)] * LOG2E
    row = lax.broadcasted_iota(jnp.int32, (A_QB, A_KB), 0)
    col = lax.broadcasted_iota(jnp.int32, (A_QB, A_KB), 1)

    for g, (_, r) in enumerate(DILATED_CONFIGS):
        for variant in range(3):
            delta = jnp.abs(col - row - variant * HALF_WINDOW)
            bias_ref[g * 3 + variant] = jnp.where(delta <= HALF_WINDOW, (-base_slope * float(r)) * delta.astype(F32), NEG)

    def chunk(bi):
        return pl.ds(pl.multiple_of(bi * A_QB, A_QB), A_QB)

    def finer_rows(bi, region):
        r0 = bi * A_QB
        base = (r0 // region) * region
        off = r0 - base
        return pl.ds(base + A_SPLIT * (off % (region // A_SPLIT)) + off // (region // A_SPLIT), A_QB, stride=A_SPLIT)

    def widen(bi, carry):
        for src, dst in zip((q_ref, k_ref, v_ref), wide0):
            dst[chunk(bi), :] = src[chunk(bi), :].astype(F32)
        return carry

    lax.fori_loop(0, n_blocks, widen, 0, unroll=4)

    def split(srcs, dsts, region):
        def body(bi, carry):
            for src, dst in zip(srcs, dsts):
                dst[chunk(bi), :] = src[finer_rows(bi, region), :].astype(dst.dtype)
            return carry
        lax.fori_loop(0, n_blocks, body, 0, unroll=4)

    split(wide0, wide1, t)
    split(wide1, fine2, t // A_SPLIT)
    sources = ((q_ref, k_ref, v_ref), wide1, fine2)

    for g in reversed(range(n_cfg)):
        r = DILATED_CONFIGS[g][1]
        assert r == A_SPLIT ** g
        n_sub = t // r
        nqb = n_sub // A_QB
        qs_ref, ks_ref, vs_ref = sources[g]

        def rows_of(bi, n_sub=n_sub, nqb=nqb):
            base = (bi // nqb) * n_sub
            q0 = (bi % nqb) * A_QB
            k0 = jnp.clip(q0 - HALF_WINDOW, 0, n_sub - A_KB)
            return pl.ds(pl.multiple_of(base + k0, HALF_WINDOW), A_KB), (q0 - k0) // HALF_WINDOW

        def scores(bi, g=g, rows_of=rows_of, qs_ref=qs_ref, ks_ref=ks_ref):
            bi = jnp.minimum(bi, n_blocks - 1)
            krows, variant = rows_of(bi)
            q = qs_ref[chunk(bi), :].astype(BF16)
            k = ks_ref[krows, :].astype(BF16)
            s = lax.dot_general(q, k, (((1,), (1,)), ((), ())), preferred_element_type=F32)
            return s + bias_ref[g * 3 + variant]

        def block(bi, s, g=g, n_sub=n_sub, rows_of=rows_of, scores=scores, vs_ref=vs_ref):
            s_next = scores(bi + 1)
            krows, _ = rows_of(bi)
            v = vs_ref[krows, :].astype(BF16)
            m = jnp.max(s, axis=-1, keepdims=True)
            p = jnp.exp2(s - m)
            den = jnp.sum(p, axis=-1, keepdims=True)
            o = jnp.dot(p.astype(BF16), v, preferred_element_type=F32) / den
            lse = jnp.broadcast_to(m + jnp.log2(den), (A_QB, HEAD_DIM))
            if g == n_cfg - 1:
                m_new, l_new, acc_new = lse, jnp.ones((A_QB, HEAD_DIM), F32), o
            else:
                m_in, l_in, acc_in = acc_sets[g % 2]
                m_old = m_in[chunk(bi), :]
                m_new = jnp.maximum(m_old, lse)
                a = jnp.exp2(m_old - m_new)
                w = jnp.exp2(lse - m_new)
                l_new = a * l_in[chunk(bi), :] + w
                acc_new = a * acc_in[chunk(bi), :] + w * o
            if g == 0:
                o_ref[chunk(bi), :] = acc_new / l_new
            else:
                dst = finer_rows(bi, n_sub * A_SPLIT)
                for ref, val in zip(acc_sets[(g - 1) % 2], (m_new, l_new, acc_new)):
                    ref[dst, :] = val
            return s_next

        lax.fori_loop(0, n_blocks, block, scores(0), unroll=8)


def _dilated_mixture(p, slopes):
    b, t, _ = p.shape
    spec = lambda off: pl.BlockSpec((None, t, HEAD_DIM), lambda i, h: (i, 0, off + h))
    seq = lambda dtype: [pltpu.VMEM((t, HEAD_DIM), dtype) for _ in range(3)]
    return pl.pallas_call(
        _dilated_mix_body,
        grid=(b, A_HEADS),
        in_specs=[pl.BlockSpec(memory_space=pltpu.SMEM), spec(0), spec(A_HEADS), spec(2 * A_HEADS)],
        out_specs=pl.BlockSpec((None, t, HEAD_DIM), lambda i, h: (i, 0, h)),
        out_shape=jax.ShapeDtypeStruct((b, t, A_W), F32),
        scratch_shapes=seq(F32) + seq(F32) + seq(BF16) + seq(F32) + seq(F32)
        + [pltpu.VMEM((3 * len(DILATED_CONFIGS), A_QB, A_KB), F32)],
        compiler_params=_params(2),
        name="dilated_mixture",
    )(slopes, p, p, p)


B_BLK = 256
B_DEN_ROWS = 16
B_AHEAD = 4


def _diff_bias_coefs():
    c = (2.0 ** (-8.0 * np.arange(1, B_HEADS + 1) / B_HEADS)).astype(np.float64) * LOG2E
    out = np.zeros((B_HEADS, 4), np.float32)
    rest = c.copy()
    for i in range(3):
        term = rest.astype(np.float32).astype(BF16).astype(np.float64)
        out[:, i] = term
        rest = rest - term
    out[:, 3] = c
    return out


def _diff_body(coef_ref, lam_ref, q_ref, k_ref, v_ref, o_ref, kaug_ref, vt_ref, *, lam_init):
    h = pl.program_id(1)
    qi = pl.program_id(2)
    nkb = k_ref.shape[0] // B_BLK
    c1, c2, c3, c = coef_ref[h, 0], coef_ref[h, 1], coef_ref[h, 2], coef_ref[h, 3]
    lane = lax.broadcasted_iota(jnp.int32, (B_BLK, LANE), 1)
    pos = lax.broadcasted_iota(jnp.int32, (B_BLK, LANE), 0).astype(F32)
    coefs = jnp.where(lane % 3 == 0, c1, jnp.where(lane % 3 == 1, c2, c3))

    @pl.when(qi == 0)
    def _():
        feat = jnp.where(lane < 3, -pos, jnp.where(lane < 6, coefs, 0.0)).astype(BF16)

        def fill(i, carry):
            rows = pl.ds(pl.multiple_of(i * B_BLK, B_BLK), B_BLK)
            kaug_ref[i, :, 0:LANE] = k_ref[rows, :]
            kaug_ref[i, :, LANE:2 * LANE] = feat
            vt_ref[i, 0:HEAD_DIM, :] = v_ref[rows, :].astype(F32).T.astype(BF16)
            vt_ref[i, HEAD_DIM:HEAD_DIM + B_DEN_ROWS, :] = jnp.ones((B_DEN_ROWS, B_BLK), BF16)
            return carry

        lax.fori_loop(0, nkb, fill, 0)

    qfeat = jnp.where(lane < 3, coefs, jnp.where(lane < 6, pos, 0.0))
    qfeat_right = qfeat.astype(BF16)
    qfeat_left = (-qfeat).astype(BF16)
    q = q_ref[...]
    zero = jnp.zeros_like(q)
    q1 = jnp.where(lane < B_QK_DIM, q, zero)
    q2 = jnp.where(lane >= B_QK_DIM, q, zero)
    nt = (((1,), (1,)), ((), ()))

    def scores(kb, qf):
        q_aug = jnp.concatenate([jnp.concatenate([q1, qf], axis=1), jnp.concatenate([q2, qf], axis=1)], axis=0)
        return lax.dot_general(kaug_ref[kb], q_aug, nt, preferred_element_type=F32)

    s = jnp.minimum(scores(qi, qfeat_right), scores(qi, qfeat_left))
    m_run = jnp.max(s, axis=0, keepdims=True)
    p = jnp.exp2(s - m_run)
    acc = jnp.dot(vt_ref[qi], p.astype(BF16), preferred_element_type=F32)

    def off_diagonal(n):
        kb = n + (n >= qi).astype(jnp.int32)
        return kb, scores(kb, jnp.where(n < qi, qfeat_left, qfeat_right))

    pending = [off_diagonal(n) for n in range(B_AHEAD)]
    for n in range(nkb - 1):
        kb, s = pending.pop(0)
        if n + B_AHEAD < nkb - 1:
            pending.append(off_diagonal(n + B_AHEAD))
        shift = (-c * B_BLK) * jnp.abs(kb - qi).astype(F32)
        m_new = jnp.maximum(m_run, jnp.max(s, axis=0, keepdims=True) + shift)
        alpha = jnp.exp2(m_run - m_new)
        p = jnp.exp2(s - (m_new - shift))
        acc = alpha * acc + jnp.dot(vt_ref[kb], p.astype(BF16), preferred_element_type=F32)
        m_run = m_new

    dl = lam_ref[...]
    lam = (jnp.exp(jnp.sum(dl[0:1, :] * dl[1:2, :], axis=-1, keepdims=True))
           - jnp.exp(jnp.sum(dl[2:3, :] * dl[3:4, :], axis=-1, keepdims=True)) + lam_init)
    o = acc[0:HEAD_DIM, :] / acc[HEAD_DIM:HEAD_DIM + 1, :]
    o_ref[...] = (o[:, 0:B_BLK] - lam * o[:, B_BLK:2 * B_BLK]).T


def _diff_attention(p, diff_lambda, lam_init):
    b, t, _ = p.shape
    nkb = t // B_BLK
    qoff, koff, voff = P_BQ // HEAD_DIM, P_BK // HEAD_DIM, P_BV // HEAD_DIM
    return pl.pallas_call(
        functools.partial(_diff_body, lam_init=lam_init),
        grid=(b, B_HEADS, nkb),
        in_specs=[
            pl.BlockSpec(memory_space=pltpu.SMEM),
            pl.BlockSpec((4, B_QK_DIM), lambda i, h, qi: (0, 0)),
            pl.BlockSpec((None, B_BLK, HEAD_DIM), lambda i, h, qi: (i, qi, qoff + h)),
            pl.BlockSpec((None, t, HEAD_DIM), lambda i, h, qi: (i, 0, koff + h)),
            pl.BlockSpec((None, t, HEAD_DIM), lambda i, h, qi: (i, 0, voff + h)),
        ],
        out_specs=pl.BlockSpec((None, B_BLK, HEAD_DIM), lambda i, h, qi: (i, qi, h)),
        out_shape=jax.ShapeDtypeStruct((b, t, B_V_W), F32),
        scratch_shapes=[pltpu.VMEM((nkb, B_BLK, 2 * LANE), BF16), pltpu.VMEM((nkb, HEAD_DIM + B_DEN_ROWS, B_BLK), BF16)],
        compiler_params=_params(3),
        name="diff_attention",
    )(jnp.asarray(_diff_bias_coefs()), diff_lambda, p, p, p)


def _log_sigmoid(x):
    return jnp.minimum(x, 0.0) - jnp.log(1.0 + jnp.exp(-jnp.abs(x)))


C_HEADS_PER_STEP = 3


def _mlstm_body(q_ref, k_ref, v_ref, og_ref, gcol_ref, grow_ref, bcol_ref, brow_ref, o_ref, c_ref):
    lc = MLSTM_CHUNK
    t = q_ref.shape[0]
    nc = t // lc
    heads = range(C_HEADS_PER_STEP)
    row = lax.broadcasted_iota(jnp.int32, (lc, lc), 0)
    col = lax.broadcasted_iota(jnp.int32, (lc, lc), 1)
    lower = col <= row
    upper = col >= row
    tri_lower = lower.astype(F32)
    tri_upper = upper.astype(F32)
    ones_col = (lax.broadcasted_iota(jnp.int32, (lc, LANE), 1) == 0).astype(BF16)
    c_ref[...] = jnp.zeros(c_ref.shape, F32)

    nt = (((1,), (1,)), ((), ()))

    def make_step(first_touch):
        def step(i, carry):
            scans = [(hh, d, ci) for hh in heads for d, ci in ((0, i), (1, nc - 1 - i))]
            ns = len(scans)
            qs, ks, vs, s_qk, s_qc, gates = [], [], [], [], [], []
            for hh, d, ci in scans:
                rows = pl.ds(pl.multiple_of(ci * lc, lc), lc)
                cols = slice(hh * HEAD_DIM, (hh + 1) * HEAD_DIM)
                q, k, v = q_ref[rows, cols], k_ref[rows, cols], v_ref[rows, cols]
                qs.append(q), ks.append(k), vs.append(v)
                s_qk.append(lax.dot_general(q, k, nt, preferred_element_type=F32))
                s_qc.append(jnp.dot(q, c_ref[2 * hh + d].astype(BF16), preferred_element_type=F32))

            for hh, d, ci in scans:
                cum_col, cum_row = (tri_lower, tri_upper) if d == 0 else (tri_upper, tri_lower)
                rows = pl.ds(pl.multiple_of(ci * lc, lc), lc)
                gc = gcol_ref[rows, :] + bcol_ref[...]
                gr = grow_ref[hh, ci] + brow_ref[hh]
                li_c = gc[:, 4 * hh + 2 * d:4 * hh + 2 * d + 1]
                lf_c = _log_sigmoid(gc[:, 4 * hh + 2 * d + 1:4 * hh + 2 * d + 2])
                li_r = gr[2 * d:2 * d + 1, :]
                lf_r = _log_sigmoid(gr[2 * d + 1:2 * d + 2, :])
                b_c = jnp.sum(cum_col * lf_r, axis=1, keepdims=True)
                b_r = jnp.sum(cum_row * lf_c, axis=0, keepdims=True)
                b_last = jnp.sum(lf_r, axis=-1, keepdims=True)
                gates.append((li_c, li_r, b_c, b_r, b_last))

            scs, m_ts, iws = [], [], []
            for n, (hh, d, ci) in enumerate(scans):
                li_c, li_r, b_c, b_r, b_last = gates[n]
                dmat = jnp.where(lower if d == 0 else upper, b_c - b_r + li_r, NEG)
                inter = b_c + carry[n]
                m_t = jnp.maximum(inter, jnp.max(dmat, axis=-1, keepdims=True))
                scs.append(s_qk[n] * jnp.exp(dmat - m_t))
                m_ts.append(m_t), iws.append(jnp.exp(inter - m_t))

            s_sv = [jnp.dot(scs[n].astype(BF16), vs[n], preferred_element_type=F32) for n in range(ns)]

            new_carry, kg_ts, decays = [], [], []
            for n, (hh, d, ci) in enumerate(scans):
                li_c, li_r, b_c, b_r, b_last = gates[n]
                num = iws[n] * s_qc[n][:, 0:HEAD_DIM] + s_sv[n]
                den = iws[n] * s_qc[n][:, HEAD_DIM:HEAD_DIM + 1] + jnp.sum(scs[n], axis=-1, keepdims=True)
                h_new = num / jnp.maximum(jnp.abs(den), jnp.exp(-m_ts[n]))
                rows = pl.ds(pl.multiple_of(ci * lc, lc), lc)
                cols = slice(hh * HEAD_DIM, (hh + 1) * HEAD_DIM)
                if first_touch:
                    o_ref[rows, cols] = h_new
                else:
                    o_ref[rows, cols] = (o_ref[rows, cols] + h_new) * jax.nn.sigmoid(og_ref[rows, cols])

                g_c = b_last - b_c + li_c
                g_r = b_last - b_r + li_r
                m_new = jnp.maximum(b_last + carry[n], jnp.max(g_r, axis=-1, keepdims=True))
                decays.append(jnp.exp(b_last + carry[n] - m_new))
                kg_ts.append((ks[n].astype(F32) * jnp.exp(g_c - m_new)).T.astype(BF16))
                new_carry.append(m_new)

            for n, (hh, d, ci) in enumerate(scans):
                v_ext = jnp.concatenate([vs[n], ones_col], axis=1)
                c_ref[2 * hh + d] = decays[n] * c_ref[2 * hh + d] + jnp.dot(kg_ts[n], v_ext, preferred_element_type=F32)
            return tuple(new_carry)
        return step

    zero = jnp.zeros((1, 1), F32)
    carry = lax.fori_loop(0, nc // 2, make_step(True), (zero,) * (2 * C_HEADS_PER_STEP))
    lax.fori_loop(nc // 2, nc, make_step(False), carry)


def _mlstm(cqk, p, pf, gate_b):
    b, t, _ = p.shape
    hb = C_HEADS_PER_STEP
    width = hb * HEAD_DIM
    groups = C_HEADS // hb
    nc = t // MLSTM_CHUNK
    cg = pf[:, :, PF_CG:PF_CG + N_GATES].reshape(b, t, 4, C_HEADS)
    gcol = cg.reshape(b, t, 4, groups, hb).transpose(0, 3, 1, 4, 2).reshape(b, groups, t, 4 * hb)
    grow = cg.reshape(b, nc, MLSTM_CHUNK, 4, C_HEADS).transpose(0, 4, 1, 3, 2)
    gb = gate_b.reshape(4, C_HEADS).T
    bcol = gb.reshape(groups, 1, 4 * hb)
    brow = gb.reshape(C_HEADS, 4, 1)
    voff, ooff = P_CV // width, PF_CO // width
    return pl.pallas_call(
        _mlstm_body,
        grid=(b, groups),
        in_specs=[
            pl.BlockSpec((None, t, width), lambda i, h: (i, 0, h)),
            pl.BlockSpec((None, t, width), lambda i, h: (i, 0, groups + h)),
            pl.BlockSpec((None, t, width), lambda i, h: (i, 0, voff + h)),
            pl.BlockSpec((None, t, width), lambda i, h: (i, 0, ooff + h)),
            pl.BlockSpec((None, None, t, 4 * hb), lambda i, h: (i, h, 0, 0)),
            pl.BlockSpec((None, hb, nc, 4, MLSTM_CHUNK), lambda i, h: (i, h, 0, 0, 0)),
            pl.BlockSpec((None, 1, 4 * hb), lambda i, h: (h, 0, 0)),
            pl.BlockSpec((hb, 4, 1), lambda i, h: (h, 0, 0)),
        ],
        out_specs=pl.BlockSpec((None, t, width), lambda i, h: (i, 0, h)),
        out_shape=jax.ShapeDtypeStruct((b, t, C_W), F32),
        scratch_shapes=[pltpu.VMEM((2 * hb, HEAD_DIM, HEAD_DIM + LANE), F32)],
        compiler_params=_params(2),
        name="mlstm",
    )(cqk, cqk, p, pf, gcol, grow, bcol, brow)


MERGE_ROWS = 32


def _merge_out_body(oa_ref, ob_ref, oc_ref, g_ref, w_ref, res_ref, o_ref, lhs_ref, *, b_scale):
    for r0 in range(0, lhs_ref.shape[0], NORM_GROUP):
        for c0 in range(r0, r0 + NORM_GROUP, MERGE_ROWS):
            rows = slice(c0, c0 + MERGE_ROWS)
            head = 0
            for src_ref, n_heads, scale in ((oa_ref, A_HEADS, None), (ob_ref, B_HEADS, b_scale), (oc_ref, C_HEADS, None)):
                for h in range(n_heads):
                    o = src_ref[rows, h * HEAD_DIM:(h + 1) * HEAD_DIM]
                    ms = jnp.mean(o * o, axis=-1, keepdims=True)
                    y = o * lax.rsqrt(ms + EPS) * g_ref[:, head * HEAD_DIM:(head + 1) * HEAD_DIM]
                    if scale is not None:
                        y = y * scale
                    lhs_ref[rows, head * HEAD_DIM:(head + 1) * HEAD_DIM] = y.astype(BF16)
                    head += 1
        group = slice(r0, r0 + NORM_GROUP)
        o_ref[group, :] = res_ref[group, :] + jnp.dot(lhs_ref[group, :], w_ref[...], preferred_element_type=F32)


def _merge_out(oa, ob, oc, g, w, res, b_scale, tm, tn):
    m, n = res.shape
    assert tn == n, "the body writes whole output rows per row group"
    row = lambda width: pl.BlockSpec((tm, width), lambda i, j: (i, 0))
    return pl.pallas_call(
        functools.partial(_merge_out_body, b_scale=b_scale),
        grid=(m // tm, n // tn),
        in_specs=[
            row(A_W), row(B_V_W), row(C_W),
            pl.BlockSpec((1, MIX_W), lambda i, j: (0, 0)),
            _weight_spec(w, (MIX_W, tn), lambda i, j: (0, j)),
            pl.BlockSpec((tm, tn), lambda i, j: (i, j)),
        ],
        out_specs=pl.BlockSpec((tm, tn), lambda i, j: (i, j)),
        out_shape=jax.ShapeDtypeStruct((m, n), F32),
        scratch_shapes=[pltpu.VMEM((tm, MIX_W), BF16)],
        compiler_params=_params(2),
        name="merge_out_proj",
    )(oa.reshape(m, A_W), ob.reshape(m, B_V_W), oc.reshape(m, C_W), g.reshape(1, MIX_W), _weight_arg(w), res)


def _xattn_out_body(q_ref, k_ref, v_ref, w_ref, res_ref, o_ref, lhs_ref):
    for r0 in range(0, lhs_ref.shape[0], NORM_GROUP):
        rows = slice(r0, r0 + NORM_GROUP)
        for h in range(X_HEADS):
            cols = slice(h * X_HEAD_DIM, (h + 1) * X_HEAD_DIM)
            s = lax.dot_general(q_ref[rows, cols], k_ref[:, cols], (((1,), (1,)), ((), ())),
                                preferred_element_type=F32) * (X_HEAD_DIM ** -0.5)
            m = jnp.max(s, axis=-1, keepdims=True)
            p = jnp.exp(s - m)
            a = p / jnp.sum(p, axis=-1, keepdims=True)
            lhs_ref[rows, cols] = jnp.dot(a.astype(BF16), v_ref[:, cols], preferred_element_type=F32).astype(BF16)
        o_ref[rows, :] = res_ref[rows, :] + jnp.dot(lhs_ref[rows, :], w_ref[...], preferred_element_type=F32)


def _xattn_out(q, kv, w, res, seq, tm, tn):
    m, n = res.shape
    assert tn == n, "the body writes whole output rows per row group"
    mem_len = kv.shape[0] // (m // seq)
    per_batch = seq // tm
    return pl.pallas_call(
        _xattn_out_body,
        grid=(m // tm, n // tn),
        in_specs=[
            pl.BlockSpec((tm, D_MODEL), lambda i, j: (i, 0)),
            pl.BlockSpec((mem_len, D_MODEL), lambda i, j: (i // per_batch, 0)),
            pl.BlockSpec((mem_len, D_MODEL), lambda i, j: (i // per_batch, 1)),
            _weight_spec(w, (D_MODEL, tn), lambda i, j: (0, j)),
            pl.BlockSpec((tm, tn), lambda i, j: (i, j)),
        ],
        out_specs=pl.BlockSpec((tm, tn), lambda i, j: (i, j)),
        out_shape=jax.ShapeDtypeStruct((m, n), F32),
        scratch_shapes=[pltpu.VMEM((tm, D_MODEL), BF16)],
        compiler_params=_params(2),
        name="xattn_out_proj",
    )(q, kv, kv, _weight_arg(w), res)


def _mlp_body(x_ref, g_ref, wu_ref, wd_ref, fg_ref, o_ref, xn_ref, *, final_norm):
    f = pl.program_id(1)

    def piece(rows):
        u = jnp.dot(xn_ref[rows, :], wu_ref[...], preferred_element_type=F32)
        a = jnp.square(jnp.maximum(u, 0.0)).astype(BF16)
        return jnp.dot(a, wd_ref[...], preferred_element_type=F32)

    @pl.when(f == 0)
    def _():
        for r0 in range(0, x_ref.shape[0], NORM_GROUP):
            rows = slice(r0, r0 + NORM_GROUP)
            _rmsnorm_group_to(x_ref, g_ref, xn_ref, r0)
            o_ref[rows, :] = x_ref[rows, :] + piece(rows)

    last = pl.num_programs(1) - 1

    @pl.when((f != 0) & ((f != last) | (not final_norm)))
    def _():
        o_ref[...] += piece(slice(None))

    if final_norm:
        @pl.when(f == last)
        def _():
            for r0 in range(0, o_ref.shape[0], NORM_GROUP):
                rows = slice(r0, r0 + NORM_GROUP)
                y = o_ref[rows, :] + piece(rows)
                for c0 in range(0, NORM_GROUP, ROW_CHUNK):
                    yc = y[c0:c0 + ROW_CHUNK, :]
                    ms = jnp.mean(yc * yc, axis=-1, keepdims=True)
                    o_ref[r0 + c0:r0 + c0 + ROW_CHUNK, :] = yc * lax.rsqrt(ms + EPS) * fg_ref[...]


def _mlp(x, g, w_up, w_down, final_g, final_norm, tm, tf):
    m, d = x.shape
    d_ff = _weight_shape(w_up)[1]
    return pl.pallas_call(
        functools.partial(_mlp_body, final_norm=final_norm),
        grid=(m // tm, d_ff // tf),
        in_specs=[
            pl.BlockSpec((tm, d), lambda i, f: (i, 0)),
            pl.BlockSpec((1, d), lambda i, f: (0, 0)),
            _weight_spec(w_up, (d, tf), lambda i, f: (0, f)),
            _weight_spec(w_down, (tf, d), lambda i, f: (f, 0)),
            pl.BlockSpec((1, d), lambda i, f: (0, 0)),
        ],
        out_specs=pl.BlockSpec((tm, d), lambda i, f: (i, 0)),
        out_shape=jax.ShapeDtypeStruct((m, d), F32),
        scratch_shapes=[pltpu.VMEM((tm, d), BF16)],
        compiler_params=_params(2),
        name="mlp",
    )(x, g.reshape(1, d), _weight_arg(w_up), _weight_arg(w_down), final_g.reshape(1, d))


def _split_in_proj(w_in):
    o = np.cumsum((0, A_W, A_W, A_W, B_QK_W, B_QK_W, B_V_W, 2 * C_W, C_W, C_W, N_GATES))
    wp = jnp.concatenate([w_in[:, o[0]:o[6]], w_in[:, o[7]:o[8]]], axis=1)
    pad = jnp.zeros((w_in.shape[0], HEAD_DIM - N_GATES), w_in.dtype)
    wpf = jnp.concatenate([w_in[:, o[6]:o[7]], w_in[:, o[8]:o[10]], pad], axis=1)
    return wp, wpf


def kernel(x, mem, norm_mix_g, w_in, conv_w, gate_b, diff_lambda, head_norm_g, w_out, norm_x_g, norm_mem_g,
           w_xq, w_xkv, w_xo, norm_mlp_g, w_up, w_down, final_norm_g):
    b, t, d = x.shape
    m = b * t
    depth = w_in.shape[0]
    h = x.reshape(m, d)
    mem2 = mem.reshape(-1, d)
    slopes_a = _alibi_slopes(A_HEADS)
    p_scale = np.ones((P_W,), np.float32)
    p_scale[0:A_W] = A_Q_SCALE
    p_scale[P_BQ:P_BK] = B_Q_SCALE
    p_scale = jnp.asarray(p_scale)
    w_out, w_xq, w_xkv, w_xo, w_up, w_down = [_to_bf16(w) for w in (w_out, w_xq, w_xkv, w_xo, w_up, w_down)]
    for layer in range(depth):
        lam_init = 0.8 - 0.6 * math.exp(-0.3 * layer)
        wp, wpf = [w.astype(BF16) for w in _split_in_proj(w_in[layer])]
        p = _norm_matmul(h, norm_mix_g[layer], wp, BF16, 1024, 2304, p_scale).reshape(b, t, P_W)
        pf = _norm_matmul(h, norm_mix_g[layer], wpf, F32, 512, PF_W).reshape(b, t, PF_W)
        cqk = _conv_silu(pf, conv_w[layer])
        oa = _dilated_mixture(p, slopes_a)
        ob = _diff_attention(p, diff_lambda[layer], lam_init)
        oc = _mlstm(cqk, p, pf, gate_b[layer])
        h = _merge_out(oa, ob, oc, head_norm_g[layer], (layer, w_out), h, 1.0 - lam_init, 512, d)

        q = _norm_matmul(h, norm_x_g[layer], (layer, w_xq), BF16, 1024, d)
        kv = _norm_matmul(mem2, norm_mem_g[layer], (layer, w_xkv), BF16, 512, 1024)
        h = _xattn_out(q, kv, (layer, w_xo), h, t, 512, d)

        h = _mlp(h, norm_mlp_g[layer], (layer, w_up), (layer, w_down), final_norm_g, layer == depth - 1, 1024, 512)
    return h.reshape(b, t, d)
```

```python
import functools
import math

import numpy as np
import jax
import jax.numpy as jnp
from jax import lax
from jax.experimental import pallas as pl
from jax.experimental.pallas import tpu as pltpu

F32 = jnp.float32
BF16 = jnp.bfloat16

D_MODEL = 2048
HEAD_DIM = 128
A_HEADS = 6
B_HEADS = 4
C_HEADS = 6
B_QK_DIM = 64
DILATED_CONFIGS = ((128, 1), (512, 4), (2048, 16))
HALF_WINDOW = 64
MLSTM_CHUNK = 128
X_HEADS = 4
X_HEAD_DIM = D_MODEL // X_HEADS
EPS = 1e-6
NEG = -1e30

A_W = A_HEADS * HEAD_DIM
B_QK_W = B_HEADS * 2 * B_QK_DIM
B_V_W = B_HEADS * HEAD_DIM
C_W = C_HEADS * HEAD_DIM
N_GATES = 4 * C_HEADS
N_MIX_HEADS = A_HEADS + B_HEADS + C_HEADS
MIX_W = N_MIX_HEADS * HEAD_DIM

P_W = 3 * A_W + 2 * B_QK_W + B_V_W + C_W
P_BQ = 3 * A_W
P_BK = P_BQ + B_QK_W
P_BV = P_BK + B_QK_W
P_CV = P_BV + B_V_W
PF_CO = 2 * C_W
PF_CG = PF_CO + C_W
PF_W = PF_CG + HEAD_DIM

LOG2E = math.log2(math.e)
A_Q_SCALE = HEAD_DIM ** -0.5 * LOG2E
B_Q_SCALE = B_QK_DIM ** -0.5 * LOG2E

LANE = 128
ROW_CHUNK = 16
NORM_GROUP = 256
CAST_BLOCK_BYTES = 8 * 1024 * 1024
VMEM_LIMIT = 56 * 1024 * 1024


def _params(n_axes, vmem=VMEM_LIMIT):
    return pltpu.CompilerParams(dimension_semantics=("arbitrary",) * n_axes, vmem_limit_bytes=vmem)


def _alibi_slopes(n_heads):
    return jnp.asarray(2.0 ** (-8.0 * np.arange(1, n_heads + 1) / n_heads), dtype=F32)


def _weight_spec(w, block, index):
    if isinstance(w, tuple):
        layer, _ = w
        return pl.BlockSpec((None,) + block, lambda *g: (layer,) + index(*g))
    return pl.BlockSpec(block, index)


def _weight_arg(w):
    return w[1] if isinstance(w, tuple) else w


def _weight_shape(w):
    return w[1].shape[1:] if isinstance(w, tuple) else w.shape


def _cast_body(x_ref, o_ref):
    o_ref[...] = x_ref[...].astype(o_ref.dtype)


def _to_bf16(w):
    depth, k, n = w.shape
    tk = k
    while tk * n * 4 > CAST_BLOCK_BYTES and tk % 16 == 0:
        tk //= 2
    spec = pl.BlockSpec((None, tk, n), lambda l, i: (l, i, 0))
    return pl.pallas_call(
        _cast_body,
        grid=(depth, k // tk),
        in_specs=[spec],
        out_specs=spec,
        out_shape=jax.ShapeDtypeStruct(w.shape, BF16),
        compiler_params=_params(2),
        name="cast_bf16",
    )(w)


def _rmsnorm_group_to(x_ref, g_ref, dst_ref, r0):
    for c0 in range(r0, r0 + NORM_GROUP, ROW_CHUNK):
        x = x_ref[c0:c0 + ROW_CHUNK, :]
        ms = jnp.mean(x * x, axis=-1, keepdims=True)
        dst_ref[c0:c0 + ROW_CHUNK, :] = (x * lax.rsqrt(ms + EPS) * g_ref[...]).astype(BF16)


def _norm_matmul_body(x_ref, g_ref, w_ref, *rest, scaled):
    s_ref = rest[0] if scaled else None
    o_ref, xn_ref = rest[-2:]

    def emit(rows):
        acc = jnp.dot(xn_ref[rows, :], w_ref[...], preferred_element_type=F32)
        if scaled:
            acc = acc * s_ref[...]
        o_ref[rows, :] = acc.astype(o_ref.dtype)

    first = pl.program_id(1) == 0

    @pl.when(first)
    def _():
        for r0 in range(0, x_ref.shape[0], NORM_GROUP):
            _rmsnorm_group_to(x_ref, g_ref, xn_ref, r0)
            emit(slice(r0, r0 + NORM_GROUP))

    @pl.when(jnp.logical_not(first))
    def _():
        emit(slice(None))


def _norm_matmul(x, g, w, out_dtype, tm, tn, col_scale=None):
    m, k = x.shape
    n = _weight_shape(w)[1]
    scaled = col_scale is not None
    in_specs = [
        pl.BlockSpec((tm, k), lambda i, j: (i, 0)),
        pl.BlockSpec((1, k), lambda i, j: (0, 0)),
        _weight_spec(w, (k, tn), lambda i, j: (0, j)),
    ]
    args = [x, g.reshape(1, k), _weight_arg(w)]
    if scaled:
        in_specs.append(pl.BlockSpec((1, tn), lambda i, j: (0, j)))
        args.append(col_scale.reshape(1, n))
    return pl.pallas_call(
        functools.partial(_norm_matmul_body, scaled=scaled),
        grid=(m // tm, n // tn),
        in_specs=in_specs,
        out_specs=pl.BlockSpec((tm, tn), lambda i, j: (i, j)),
        out_shape=jax.ShapeDtypeStruct((m, n), out_dtype),
        scratch_shapes=[pltpu.VMEM((tm, k), BF16)],
        compiler_params=_params(2),
        name="norm_matmul",
    )(*args)


CONV_COLS = 256
CONV_PAD = 8


def _conv_silu_body(x_ref, w_ref, o_ref, pad_ref):
    t = x_ref.shape[0]
    zeros = jnp.zeros((CONV_PAD, CONV_COLS), F32)
    pad_ref[0:CONV_PAD, :] = zeros
    pad_ref[CONV_PAD + t:2 * CONV_PAD + t, :] = zeros
    scale = jnp.where(pl.program_id(1) < C_W // CONV_COLS, HEAD_DIM ** -0.5, 1.0).astype(F32)

    def copy(i, carry):
        r0 = pl.multiple_of(i * LANE, LANE)
        pad_ref[pl.ds(CONV_PAD + r0, LANE), :] = x_ref[pl.ds(r0, LANE), :]
        return carry

    lax.fori_loop(0, t // LANE, copy, 0)
    w = w_ref[...]
    row = lax.broadcasted_iota(jnp.int32, (LANE, CONV_COLS), 0)

    def body(i, carry):
        r0 = pl.multiple_of(i * LANE, LANE)
        cur = pad_ref[pl.ds(CONV_PAD + r0, LANE), :]
        before = pad_ref[pl.ds(r0, CONV_PAD), :]
        after = pad_ref[pl.ds(CONV_PAD + LANE + r0, CONV_PAD), :]
        prev = jnp.where(row == 0, before[CONV_PAD - 1:CONV_PAD, :], pltpu.roll(cur, 1, axis=0))
        nxt = jnp.where(row == LANE - 1, after[0:1, :], pltpu.roll(cur, LANE - 1, axis=0))
        y = prev * w[0:1, :] + cur * w[1:2, :] + nxt * w[2:3, :]
        y = y * jax.nn.sigmoid(y)
        o_ref[pl.ds(r0, LANE), :] = (y * scale).astype(BF16)
        return carry

    lax.fori_loop(0, t // LANE, body, 0)


def _conv_silu(pf, conv_w):
    b, t, _ = pf.shape
    ncol = 2 * C_W // CONV_COLS
    return pl.pallas_call(
        _conv_silu_body,
        grid=(b, ncol),
        in_specs=[
            pl.BlockSpec((None, t, CONV_COLS), lambda i, j: (i, 0, j)),
            pl.BlockSpec((3, CONV_COLS), lambda i, j: (0, j)),
        ],
        out_specs=pl.BlockSpec((None, t, CONV_COLS), lambda i, j: (i, 0, j)),
        out_shape=jax.ShapeDtypeStruct((b, t, 2 * C_W), BF16),
        scratch_shapes=[pltpu.VMEM((t + 2 * CONV_PAD, CONV_COLS), F32)],
        compiler_params=_params(2),
        name="conv_silu",
    )(pf, conv_w)


A_QB = 128
A_KB = A_QB + 2 * HALF_WINDOW


A_SPLIT = 4


def _dilated_mix_body(slopes_ref, q_ref, k_ref, v_ref, o_ref, *scratch):
    n_cfg = len(DILATED_CONFIGS)
    wide0, wide1, fine2 = scratch[0:3], scratch[3:6], scratch[6:9]
    acc_sets = (scratch[9:12], scratch[12:15])
    bias_ref = scratch[15]
    t = q_ref.shape[0]
    n_blocks = t // A_QB
    base_slope = slopes_ref[pl.program_id(1)] * LOG2E
    row = lax.broadcasted_iota(jnp.int32, (A_QB, A_KB), 0)
    col = lax.broadcasted_iota(jnp.int32, (A_QB, A_KB), 1)

    for g, (_, r) in enumerate(DILATED_CONFIGS):
        for variant in range(3):
            delta = jnp.abs(col - row - variant * HALF_WINDOW)
            bias_ref[g * 3 + variant] = jnp.where(delta <= HALF_WINDOW, (-base_slope * float(r)) * delta.astype(F32), NEG)

    def chunk(bi):
        return pl.ds(pl.multiple_of(bi * A_QB, A_QB), A_QB)

    def finer_rows(bi, region):
        r0 = bi * A_QB
        base = (r0 // region) * region
        off = r0 - base
        return pl.ds(base + A_SPLIT * (off % (region // A_SPLIT)) + off // (region // A_SPLIT), A_QB, stride=A_SPLIT)

    def widen(bi, carry):
        for src, dst in zip((q_ref, k_ref, v_ref), wide0):
            dst[chunk(bi), :] = src[chunk(bi), :].astype(F32)
        return carry

    lax.fori_loop(0, n_blocks, widen, 0, unroll=4)

    def split(srcs, dsts, region):
        def body(bi, carry):
            for src, dst in zip(srcs, dsts):
                dst[chunk(bi), :] = src[finer_rows(bi, region), :].astype(dst.dtype)
            return carry
        lax.fori_loop(0, n_blocks, body, 0, unroll=4)

    split(wide0, wide1, t)
    split(wide1, fine2, t // A_SPLIT)
    sources = ((q_ref, k_ref, v_ref), wide1, fine2)

    for g in reversed(range(n_cfg)):
        r = DILATED_CONFIGS[g][1]
        assert r == A_SPLIT ** g
        n_sub = t // r
        nqb = n_sub // A_QB
        qs_ref, ks_ref, vs_ref = sources[g]

        def rows_of(bi, n_sub=n_sub, nqb=nqb):
            base = (bi // nqb) * n_sub
            q0 = (bi % nqb) * A_QB
            k0 = jnp.clip(q0 - HALF_WINDOW, 0, n_sub - A_KB)
            return pl.ds(pl.multiple_of(base + k0, HALF_WINDOW), A_KB), (q0 - k0) // HALF_WINDOW

        def scores(bi, g=g, rows_of=rows_of, qs_ref=qs_ref, ks_ref=ks_ref):
            bi = jnp.minimum(bi, n_blocks - 1)
            krows, variant = rows_of(bi)
            q = qs_ref[chunk(bi), :].astype(BF16)
            k = ks_ref[krows, :].astype(BF16)
            s = lax.dot_general(q, k, (((1,), (1,)), ((), ())), preferred_element_type=F32)
            return s + bias_ref[g * 3 + variant]

        def block(bi, s, g=g, n_sub=n_sub, rows_of=rows_of, scores=scores, vs_ref=vs_ref):
            s_next = scores(bi + 1)
            krows, _ = rows_of(bi)
            v = vs_ref[krows, :].astype(BF16)
            m = jnp.max(s, axis=-1, keepdims=True)
            p = jnp.exp2(s - m)
            den = jnp.sum(p, axis=-1, keepdims=True)
            o = jnp.dot(p.astype(BF16), v, preferred_element_type=F32) / den
            lse = jnp.broadcast_to(m + jnp.log2(den), (A_QB, HEAD_DIM))
            if g == n_cfg - 1:
                m_new, l_new, acc_new = lse, jnp.ones((A_QB, HEAD_DIM), F32), o
            else:
                m_in, l_in, acc_in = acc_sets[g % 2]
                m_old = m_in[chunk(bi), :]
                m_new = jnp.maximum(m_old, lse)
                a = jnp.exp2(m_old - m_new)
                w = jnp.exp2(lse - m_new)
                l_new = a * l_in[chunk(bi), :] + w
                acc_new = a * acc_in[chunk(bi), :] + w * o
            if g == 0:
                o_ref[chunk(bi), :] = acc_new / l_new
            else:
                dst = finer_rows(bi, n_sub * A_SPLIT)
                for ref, val in zip(acc_sets[(g - 1) % 2], (m_new, l_new, acc_new)):
                    ref[dst, :] = val
            return s_next

        lax.fori_loop(0, n_blocks, block, scores(0), unroll=8)


def _dilated_mixture(p, slopes):
    b, t, _ = p.shape
    spec = lambda off: pl.BlockSpec((None, t, HEAD_DIM), lambda i, h: (i, 0, off + h))
    seq = lambda dtype: [pltpu.VMEM((t, HEAD_DIM), dtype) for _ in range(3)]
    return pl.pallas_call(
        _dilated_mix_body,
        grid=(b, A_HEADS),
        in_specs=[pl.BlockSpec(memory_space=pltpu.SMEM), spec(0), spec(A_HEADS), spec(2 * A_HEADS)],
        out_specs=pl.BlockSpec((None, t, HEAD_DIM), lambda i, h: (i, 0, h)),
        out_shape=jax.ShapeDtypeStruct((b, t, A_W), F32),
        scratch_shapes=seq(F32) + seq(F32) + seq(BF16) + seq(F32) + seq(F32)
        + [pltpu.VMEM((3 * len(DILATED_CONFIGS), A_QB, A_KB), F32)],
        compiler_params=_params(2),
        name="dilated_mixture",
    )(slopes, p, p, p)


B_BLK = 256
B_DEN_ROWS = 16
B_AHEAD = 15


def _diff_bias_coefs():
    c = (2.0 ** (-8.0 * np.arange(1, B_HEADS + 1) / B_HEADS)).astype(np.float64) * LOG2E
    out = np.zeros((B_HEADS, 4), np.float32)
    rest = c.copy()
    for i in range(3):
        term = rest.astype(np.float32).astype(BF16).astype(np.float64)
        out[:, i] = term
        rest = rest - term
    out[:, 3] = c
    return out


def _diff_body(coef_ref, lam_ref, q_ref, k_ref, v_ref, o_ref, kaug_ref, vt_ref, *, lam_init):
    h = pl.program_id(1)
    qi = pl.program_id(2)
    nkb = k_ref.shape[0] // B_BLK
    c1, c2, c3, c = coef_ref[h, 0], coef_ref[h, 1], coef_ref[h, 2], coef_ref[h, 3]
    lane = lax.broadcasted_iota(jnp.int32, (B_BLK, LANE), 1)
    pos = lax.broadcasted_iota(jnp.int32, (B_BLK, LANE), 0).astype(F32)
    coefs = jnp.where(lane % 3 == 0, c1, jnp.where(lane % 3 == 1, c2, c3))

    @pl.when(qi == 0)
    def _():
        feat = jnp.where(lane < 3, -pos, jnp.where(lane < 6, coefs, 0.0)).astype(BF16)

        def fill(i, carry):
            rows = pl.ds(pl.multiple_of(i * B_BLK, B_BLK), B_BLK)
            kaug_ref[i, :, 0:LANE] = k_ref[rows, :]
            kaug_ref[i, :, LANE:2 * LANE] = feat
            vt_ref[i, 0:HEAD_DIM, :] = v_ref[rows, :].astype(F32).T.astype(BF16)
            vt_ref[i, HEAD_DIM:HEAD_DIM + B_DEN_ROWS, :] = jnp.ones((B_DEN_ROWS, B_BLK), BF16)
            return carry

        lax.fori_loop(0, nkb, fill, 0)

    qfeat = jnp.where(lane < 3, coefs, jnp.where(lane < 6, pos, 0.0))
    qfeat_right = qfeat.astype(BF16)
    qfeat_left = (-qfeat).astype(BF16)
    q = q_ref[...]
    zero = jnp.zeros_like(q)
    q1 = jnp.where(lane < B_QK_DIM, q, zero)
    q2 = jnp.where(lane >= B_QK_DIM, q, zero)
    nt = (((1,), (1,)), ((), ()))

    def scores(kb, qf):
        q_aug = jnp.concatenate([jnp.concatenate([q1, qf], axis=1), jnp.concatenate([q2, qf], axis=1)], axis=0)
        return lax.dot_general(kaug_ref[kb], q_aug, nt, preferred_element_type=F32)

    s = jnp.minimum(scores(qi, qfeat_right), scores(qi, qfeat_left))
    m_run = jnp.max(s, axis=0, keepdims=True)
    p = jnp.exp2(s - m_run)
    acc = jnp.dot(vt_ref[qi], p.astype(BF16), preferred_element_type=F32)

    def off_diagonal(n):
        kb = n + (n >= qi).astype(jnp.int32)
        return kb, scores(kb, jnp.where(n < qi, qfeat_left, qfeat_right))

    pending = [off_diagonal(n) for n in range(B_AHEAD)]
    for n in range(nkb - 1):
        kb, s = pending.pop(0)
        if n + B_AHEAD < nkb - 1:
            pending.append(off_diagonal(n + B_AHEAD))
        shift = (-c * B_BLK) * jnp.abs(kb - qi).astype(F32)
        m_new = jnp.maximum(m_run, jnp.max(s, axis=0, keepdims=True) + shift)
        alpha = jnp.exp2(m_run - m_new)
        p = jnp.exp2(s - (m_new - shift))
        acc = alpha * acc + jnp.dot(vt_ref[kb], p.astype(BF16), preferred_element_type=F32)
        m_run = m_new

    dl = lam_ref[...]
    lam = (jnp.exp(jnp.sum(dl[0:1, :] * dl[1:2, :], axis=-1, keepdims=True))
           - jnp.exp(jnp.sum(dl[2:3, :] * dl[3:4, :], axis=-1, keepdims=True)) + lam_init)
    o = acc[0:HEAD_DIM, :] / acc[HEAD_DIM:HEAD_DIM + 1, :]
    o_ref[...] = (o[:, 0:B_BLK] - lam * o[:, B_BLK:2 * B_BLK]).T


def _diff_attention(p, diff_lambda, lam_init):
    b, t, _ = p.shape
    nkb = t // B_BLK
    qoff, koff, voff = P_BQ // HEAD_DIM, P_BK // HEAD_DIM, P_BV // HEAD_DIM
    return pl.pallas_call(
        functools.partial(_diff_body, lam_init=lam_init),
        grid=(b, B_HEADS, nkb),
        in_specs=[
            pl.BlockSpec(memory_space=pltpu.SMEM),
            pl.BlockSpec((4, B_QK_DIM), lambda i, h, qi: (0, 0)),
            pl.BlockSpec((None, B_BLK, HEAD_DIM), lambda i, h, qi: (i, qi, qoff + h)),
            pl.BlockSpec((None, t, HEAD_DIM), lambda i, h, qi: (i, 0, koff + h)),
            pl.BlockSpec((None, t, HEAD_DIM), lambda i, h, qi: (i, 0, voff + h)),
        ],
        out_specs=pl.BlockSpec((None, B_BLK, HEAD_DIM), lambda i, h, qi: (i, qi, h)),
        out_shape=jax.ShapeDtypeStruct((b, t, B_V_W), F32),
        scratch_shapes=[pltpu.VMEM((nkb, B_BLK, 2 * LANE), BF16), pltpu.VMEM((nkb, HEAD_DIM + B_DEN_ROWS, B_BLK), BF16)],
        compiler_params=_params(3),
        name="diff_attention",
    )(jnp.asarray(_diff_bias_coefs()), diff_lambda, p, p, p)


def _log_sigmoid(x):
    return jnp.minimum(x, 0.0) - jnp.log(1.0 + jnp.exp(-jnp.abs(x)))


C_HEADS_PER_STEP = 3


def _mlstm_body(q_ref, k_ref, v_ref, og_ref, gcol_ref, grow_ref, bcol_ref, brow_ref, o_ref, c_ref):
    lc = MLSTM_CHUNK
    t = q_ref.shape[0]
    nc = t // lc
    heads = range(C_HEADS_PER_STEP)
    row = lax.broadcasted_iota(jnp.int32, (lc, lc), 0)
    col = lax.broadcasted_iota(jnp.int32, (lc, lc), 1)
    lower = col <= row
    upper = col >= row
    tri_lower = lower.astype(F32)
    tri_upper = upper.astype(F32)
    ones_col = (lax.broadcasted_iota(jnp.int32, (lc, LANE), 1) == 0).astype(BF16)
    c_ref[...] = jnp.zeros(c_ref.shape, F32)

    nt = (((1,), (1,)), ((), ()))

    def make_step(first_touch):
        def step(i, carry):
            scans = [(hh, d, ci) for hh in heads for d, ci in ((0, i), (1, nc - 1 - i))]
            ns = len(scans)
            qs, ks, vs, s_qk, s_qc, gates = [], [], [], [], [], []
            for hh, d, ci in scans:
                rows = pl.ds(pl.multiple_of(ci * lc, lc), lc)
                cols = slice(hh * HEAD_DIM, (hh + 1) * HEAD_DIM)
                q, k, v = q_ref[rows, cols], k_ref[rows, cols], v_ref[rows, cols]
                qs.append(q), ks.append(k), vs.append(v)
                s_qk.append(lax.dot_general(q, k, nt, preferred_element_type=F32))
                s_qc.append(jnp.dot(q, c_ref[2 * hh + d].astype(BF16), preferred_element_type=F32))

            for hh, d, ci in scans:
                cum_col, cum_row = (tri_lower, tri_upper) if d == 0 else (tri_upper, tri_lower)
                rows = pl.ds(pl.multiple_of(ci * lc, lc), lc)
                gc = gcol_ref[rows, :] + bcol_ref[...]
                gr = grow_ref[hh, ci] + brow_ref[hh]
                li_c = gc[:, 4 * hh + 2 * d:4 * hh + 2 * d + 1]
                lf_c = _log_sigmoid(gc[:, 4 * hh + 2 * d + 1:4 * hh + 2 * d + 2])
                li_r = gr[2 * d:2 * d + 1, :]
                lf_r = _log_sigmoid(gr[2 * d + 1:2 * d + 2, :])
                b_c = jnp.sum(cum_col * lf_r, axis=1, keepdims=True)
                b_r = jnp.sum(cum_row * lf_c, axis=0, keepdims=True)
                b_last = jnp.sum(lf_r, axis=-1, keepdims=True)
                gates.append((li_c, li_r, b_c, b_r, b_last))

            scs, m_ts, iws = [], [], []
            for n, (hh, d, ci) in enumerate(scans):
                li_c, li_r, b_c, b_r, b_last = gates[n]
                dmat = jnp.where(lower if d == 0 else upper, b_c - b_r + li_r, NEG)
                inter = b_c + carry[n]
                m_t = jnp.maximum(inter, jnp.max(dmat, axis=-1, keepdims=True))
                scs.append(s_qk[n] * jnp.exp(dmat - m_t))
                m_ts.append(m_t), iws.append(jnp.exp(inter - m_t))

            s_sv = [jnp.dot(scs[n].astype(BF16), vs[n], preferred_element_type=F32) for n in range(ns)]

            new_carry, kg_ts, decays = [], [], []
            for n, (hh, d, ci) in enumerate(scans):
                li_c, li_r, b_c, b_r, b_last = gates[n]
                num = iws[n] * s_qc[n][:, 0:HEAD_DIM] + s_sv[n]
                den = iws[n] * s_qc[n][:, HEAD_DIM:HEAD_DIM + 1] + jnp.sum(scs[n], axis=-1, keepdims=True)
                h_new = num / jnp.maximum(jnp.abs(den), jnp.exp(-m_ts[n]))
                rows = pl.ds(pl.multiple_of(ci * lc, lc), lc)
                cols = slice(hh * HEAD_DIM, (hh + 1) * HEAD_DIM)
                if first_touch:
                    o_ref[rows, cols] = h_new
                else:
                    o_ref[rows, cols] = (o_ref[rows, cols] + h_new) * jax.nn.sigmoid(og_ref[rows, cols])

                g_c = b_last - b_c + li_c
                g_r = b_last - b_r + li_r
                m_new = jnp.maximum(b_last + carry[n], jnp.max(g_r, axis=-1, keepdims=True))
                decays.append(jnp.exp(b_last + carry[n] - m_new))
                kg_ts.append((ks[n].astype(F32) * jnp.exp(g_c - m_new)).T.astype(BF16))
                new_carry.append(m_new)

            for n, (hh, d, ci) in enumerate(scans):
                v_ext = jnp.concatenate([vs[n], ones_col], axis=1)
                c_ref[2 * hh + d] = decays[n] * c_ref[2 * hh + d] + jnp.dot(kg_ts[n], v_ext, preferred_element_type=F32)
            return tuple(new_carry)
        return step

    zero = jnp.zeros((1, 1), F32)
    carry = lax.fori_loop(0, nc // 2, make_step(True), (zero,) * (2 * C_HEADS_PER_STEP))
    lax.fori_loop(nc // 2, nc, make_step(False), carry)


def _mlstm(cqk, p, pf, gate_b):
    b, t, _ = p.shape
    hb = C_HEADS_PER_STEP
    width = hb * HEAD_DIM
    groups = C_HEADS // hb
    nc = t // MLSTM_CHUNK
    cg = pf[:, :, PF_CG:PF_CG + N_GATES].reshape(b, t, 4, C_HEADS)
    gcol = cg.reshape(b, t, 4, groups, hb).transpose(0, 3, 1, 4, 2).reshape(b, groups, t, 4 * hb)
    grow = cg.reshape(b, nc, MLSTM_CHUNK, 4, C_HEADS).transpose(0, 4, 1, 3, 2)
    gb = gate_b.reshape(4, C_HEADS).T
    bcol = gb.reshape(groups, 1, 4 * hb)
    brow = gb.reshape(C_HEADS, 4, 1)
    voff, ooff = P_CV // width, PF_CO // width
    return pl.pallas_call(
        _mlstm_body,
        grid=(b, groups),
        in_specs=[
            pl.BlockSpec((None, t, width), lambda i, h: (i, 0, h)),
            pl.BlockSpec((None, t, width), lambda i, h: (i, 0, groups + h)),
            pl.BlockSpec((None, t, width), lambda i, h: (i, 0, voff + h)),
            pl.BlockSpec((None, t, width), lambda i, h: (i, 0, ooff + h)),
            pl.BlockSpec((None, None, t, 4 * hb), lambda i, h: (i, h, 0, 0)),
            pl.BlockSpec((None, hb, nc, 4, MLSTM_CHUNK), lambda i, h: (i, h, 0, 0, 0)),
            pl.BlockSpec((None, 1, 4 * hb), lambda i, h: (h, 0, 0)),
            pl.BlockSpec((hb, 4, 1), lambda i, h: (h, 0, 0)),
        ],
        out_specs=pl.BlockSpec((None, t, width), lambda i, h: (i, 0, h)),
        out_shape=jax.ShapeDtypeStruct((b, t, C_W), F32),
        scratch_shapes=[pltpu.VMEM((2 * hb, HEAD_DIM, HEAD_DIM + LANE), F32)],
        compiler_params=_params(2),
        name="mlstm",
    )(cqk, cqk, p, pf, gcol, grow, bcol, brow)


MERGE_ROWS = 32


def _merge_out_body(oa_ref, ob_ref, oc_ref, g_ref, w_ref, res_ref, o_ref, lhs_ref, *, b_scale):
    for r0 in range(0, lhs_ref.shape[0], NORM_GROUP):
        for c0 in range(r0, r0 + NORM_GROUP, MERGE_ROWS):
            rows = slice(c0, c0 + MERGE_ROWS)
            head = 0
            for src_ref, n_heads, scale in ((oa_ref, A_HEADS, None), (ob_ref, B_HEADS, b_scale), (oc_ref, C_HEADS, None)):
                for h in range(n_heads):
                    o = src_ref[rows, h * HEAD_DIM:(h + 1) * HEAD_DIM]
                    ms = jnp.mean(o * o, axis=-1, keepdims=True)
                    y = o * lax.rsqrt(ms + EPS) * g_ref[:, head * HEAD_DIM:(head + 1) * HEAD_DIM]
                    if scale is not None:
                        y = y * scale
                    lhs_ref[rows, head * HEAD_DIM:(head + 1) * HEAD_DIM] = y.astype(BF16)
                    head += 1
        group = slice(r0, r0 + NORM_GROUP)
        o_ref[group, :] = res_ref[group, :] + jnp.dot(lhs_ref[group, :], w_ref[...], preferred_element_type=F32)


def _merge_out(oa, ob, oc, g, w, res, b_scale, tm, tn):
    m, n = res.shape
    assert tn == n, "the body writes whole output rows per row group"
    row = lambda width: pl.BlockSpec((tm, width), lambda i, j: (i, 0))
    return pl.pallas_call(
        functools.partial(_merge_out_body, b_scale=b_scale),
        grid=(m // tm, n // tn),
        in_specs=[
            row(A_W), row(B_V_W), row(C_W),
            pl.BlockSpec((1, MIX_W), lambda i, j: (0, 0)),
            _weight_spec(w, (MIX_W, tn), lambda i, j: (0, j)),
            pl.BlockSpec((tm, tn), lambda i, j: (i, j)),
        ],
        out_specs=pl.BlockSpec((tm, tn), lambda i, j: (i, j)),
        out_shape=jax.ShapeDtypeStruct((m, n), F32),
        scratch_shapes=[pltpu.VMEM((tm, MIX_W), BF16)],
        compiler_params=_params(2),
        name="merge_out_proj",
    )(oa.reshape(m, A_W), ob.reshape(m, B_V_W), oc.reshape(m, C_W), g.reshape(1, MIX_W), _weight_arg(w), res)


def _xattn_out_body(q_ref, k_ref, v_ref, w_ref, res_ref, o_ref, lhs_ref):
    for r0 in range(0, lhs_ref.shape[0], NORM_GROUP):
        rows = slice(r0, r0 + NORM_GROUP)
        for h in range(X_HEADS):
            cols = slice(h * X_HEAD_DIM, (h + 1) * X_HEAD_DIM)
            s = lax.dot_general(q_ref[rows, cols], k_ref[:, cols], (((1,), (1,)), ((), ())),
                                preferred_element_type=F32) * (X_HEAD_DIM ** -0.5)
            m = jnp.max(s, axis=-1, keepdims=True)
            p = jnp.exp(s - m)
            a = p / jnp.sum(p, axis=-1, keepdims=True)
            lhs_ref[rows, cols] = jnp.dot(a.astype(BF16), v_ref[:, cols], preferred_element_type=F32).astype(BF16)
        o_ref[rows, :] = res_ref[rows, :] + jnp.dot(lhs_ref[rows, :], w_ref[...], preferred_element_type=F32)


def _xattn_out(q, kv, w, res, seq, tm, tn):
    m, n = res.shape
    assert tn == n, "the body writes whole output rows per row group"
    mem_len = kv.shape[0] // (m // seq)
    per_batch = seq // tm
    return pl.pallas_call(
        _xattn_out_body,
        grid=(m // tm, n // tn),
        in_specs=[
            pl.BlockSpec((tm, D_MODEL), lambda i, j: (i, 0)),
            pl.BlockSpec((mem_len, D_MODEL), lambda i, j: (i // per_batch, 0)),
            pl.BlockSpec((mem_len, D_MODEL), lambda i, j: (i // per_batch, 1)),
            _weight_spec(w, (D_MODEL, tn), lambda i, j: (0, j)),
            pl.BlockSpec((tm, tn), lambda i, j: (i, j)),
        ],
        out_specs=pl.BlockSpec((tm, tn), lambda i, j: (i, j)),
        out_shape=jax.ShapeDtypeStruct((m, n), F32),
        scratch_shapes=[pltpu.VMEM((tm, D_MODEL), BF16)],
        compiler_params=_params(2),
        name="xattn_out_proj",
    )(q, kv, kv, _weight_arg(w), res)


def _mlp_body(x_ref, g_ref, wu_ref, wd_ref, fg_ref, o_ref, xn_ref, *, final_norm):
    f = pl.program_id(1)

    def piece(rows):
        u = jnp.dot(xn_ref[rows, :], wu_ref[...], preferred_element_type=F32)
        a = jnp.square(jnp.maximum(u, 0.0)).astype(BF16)
        return jnp.dot(a, wd_ref[...], preferred_element_type=F32)

    @pl.when(f == 0)
    def _():
        for r0 in range(0, x_ref.shape[0], NORM_GROUP):
            rows = slice(r0, r0 + NORM_GROUP)
            _rmsnorm_group_to(x_ref, g_ref, xn_ref, r0)
            o_ref[rows, :] = x_ref[rows, :] + piece(rows)

    last = pl.num_programs(1) - 1

    @pl.when((f != 0) & ((f != last) | (not final_norm)))
    def _():
        o_ref[...] += piece(slice(None))

    if final_norm:
        @pl.when(f == last)
        def _():
            for r0 in range(0, o_ref.shape[0], NORM_GROUP):
                rows = slice(r0, r0 + NORM_GROUP)
                y = o_ref[rows, :] + piece(rows)
                for c0 in range(0, NORM_GROUP, ROW_CHUNK):
                    yc = y[c0:c0 + ROW_CHUNK, :]
                    ms = jnp.mean(yc * yc, axis=-1, keepdims=True)
                    o_ref[r0 + c0:r0 + c0 + ROW_CHUNK, :] = yc * lax.rsqrt(ms + EPS) * fg_ref[...]


def _mlp(x, g, w_up, w_down, final_g, final_norm, tm, tf):
    m, d = x.shape
    d_ff = _weight_shape(w_up)[1]
    return pl.pallas_call(
        functools.partial(_mlp_body, final_norm=final_norm),
        grid=(m // tm, d_ff // tf),
        in_specs=[
            pl.BlockSpec((tm, d), lambda i, f: (i, 0)),
            pl.BlockSpec((1, d), lambda i, f: (0, 0)),
            _weight_spec(w_up, (d, tf), lambda i, f: (0, f)),
            _weight_spec(w_down, (tf, d), lambda i, f: (f, 0)),
            pl.BlockSpec((1, d), lambda i, f: (0, 0)),
        ],
        out_specs=pl.BlockSpec((tm, d), lambda i, f: (i, 0)),
        out_shape=jax.ShapeDtypeStruct((m, d), F32),
        scratch_shapes=[pltpu.VMEM((tm, d), BF16)],
        compiler_params=_params(2),
        name="mlp",
    )(x, g.reshape(1, d), _weight_arg(w_up), _weight_arg(w_down), final_g.reshape(1, d))


def _split_in_proj(w_in):
    o = np.cumsum((0, A_W, A_W, A_W, B_QK_W, B_QK_W, B_V_W, 2 * C_W, C_W, C_W, N_GATES))
    wp = jnp.concatenate([w_in[:, o[0]:o[6]], w_in[:, o[7]:o[8]]], axis=1)
    pad = jnp.zeros((w_in.shape[0], HEAD_DIM - N_GATES), w_in.dtype)
    wpf = jnp.concatenate([w_in[:, o[6]:o[7]], w_in[:, o[8]:o[10]], pad], axis=1)
    return wp, wpf


def kernel(x, mem, norm_mix_g, w_in, conv_w, gate_b, diff_lambda, head_norm_g, w_out, norm_x_g, norm_mem_g,
           w_xq, w_xkv, w_xo, norm_mlp_g, w_up, w_down, final_norm_g):
    b, t, d = x.shape
    m = b * t
    depth = w_in.shape[0]
    h = x.reshape(m, d)
    mem2 = mem.reshape(-1, d)
    slopes_a = _alibi_slopes(A_HEADS)
    p_scale = np.ones((P_W,), np.float32)
    p_scale[0:A_W] = A_Q_SCALE
    p_scale[P_BQ:P_BK] = B_Q_SCALE
    p_scale = jnp.asarray(p_scale)
    w_out, w_xq, w_xkv, w_xo, w_up, w_down = [_to_bf16(w) for w in (w_out, w_xq, w_xkv, w_xo, w_up, w_down)]
    for layer in range(depth):
        lam_init = 0.8 - 0.6 * math.exp(-0.3 * layer)
        wp, wpf = [w.astype(BF16) for w in _split_in_proj(w_in[layer])]
        p = _norm_matmul(h, norm_mix_g[layer], wp, BF16, 1024, 2304, p_scale).reshape(b, t, P_W)
        pf = _norm_matmul(h, norm_mix_g[layer], wpf, F32, 512, PF_W).reshape(b, t, PF_W)
        cqk = _conv_silu(pf, conv_w[layer])
        oa = _dilated_mixture(p, slopes_a)
        ob = _diff_attention(p, diff_lambda[layer], lam_init)
        oc = _mlstm(cqk, p, pf, gate_b[layer])
        h = _merge_out(oa, ob, oc, head_norm_g[layer], (layer, w_out), h, 1.0 - lam_init, 512, d)

        q = _norm_matmul(h, norm_x_g[layer], (layer, w_xq), BF16, 1024, d)
        kv = _norm_matmul(mem2, norm_mem_g[layer], (layer, w_xkv), BF16, 512, 1024)
        h = _xattn_out(q, kv, (layer, w_xo), h, t, 512, d)

        h = _mlp(h, norm_mlp_g[layer], (layer, w_up), (layer, w_down), final_norm_g, layer == depth - 1, 1024, 512)
    return h.reshape(b, t, d)
```
